```python
import jax, jax.numpy as jnp
from jax import lax
import numpy as np

D_MODEL = 1024
BATCH = 4
SEQ = 8192
DEPTH = 1

CHUNK = 64
MIX_WIDTH = D_MODEL
CONV_WIDTH = MIX_WIDTH // 2
CONV_GROUPS = 8
CONV_K = 3
GMLP_WIDTH = MIX_WIDTH - CONV_WIDTH
GMLP_HEADS = 8
GMLP_HEAD_DIM = GMLP_WIDTH // GMLP_HEADS
GMLP_BLOCK = 128
D_FF = 4 * D_MODEL
IN_PROJ = 3 * CONV_WIDTH + 2 * GMLP_WIDTH
EPS = 1e-6

kernel_name = "hybrid_conv_gmlp_parallel_block"


def rmsnorm(x, g):
    xf = x.astype(jnp.float32)
    y = xf * lax.rsqrt(jnp.mean(xf * xf, axis=-1, keepdims=True) + EPS)
    return (y * g.astype(jnp.float32)).astype(x.dtype)


def layernorm(x, g, b):
    xf = x.astype(jnp.float32)
    mu = jnp.mean(xf, axis=-1, keepdims=True)
    xc = xf - mu
    y = xc * lax.rsqrt(jnp.mean(xc * xc, axis=-1, keepdims=True) + EPS)
    return (y * g.astype(jnp.float32) + b.astype(jnp.float32)).astype(x.dtype)


def causal_depthwise_conv(z, w):
    s = z.shape[1]
    zp = jnp.pad(z, ((0, 0), (CONV_K - 1, 0), (0, 0)))
    y = zp[:, 0:s] * w[:, 0]
    for k in range(1, CONV_K):
        y = y + zp[:, k:k + s] * w[:, k]
    return y


def spatial_gating(v, w_s, b_s):
    bsz, s, _ = v.shape
    nb = s // GMLP_BLOCK
    idx = jnp.arange(GMLP_BLOCK)
    mask = (idx[None, :] // CHUNK) <= (idx[:, None] // CHUNK)
    w = jnp.where(mask[None], w_s, jnp.zeros_like(w_s))
    vb = v.reshape(bsz, nb, GMLP_BLOCK, GMLP_HEADS, GMLP_HEAD_DIM)
    out = jnp.einsum('hij,bnjhd->bnihd', w, vb)
    out = out + jnp.transpose(b_s)[None, None, :, :, None]
    return out.reshape(bsz, s, GMLP_WIDTH)


def setup_inputs(seed: int = 0) -> dict:
    key = jax.random.key(seed)
    ks = jax.random.split(key, 16)
    f32 = jnp.float32
    nrm = lambda k, shape, scale: (jax.random.normal(k, shape, f32) * scale)
    return {
        "x": jax.random.normal(ks[0], (BATCH, SEQ, D_MODEL), f32),
        "norm1_g": 1.0 + nrm(ks[1], (DEPTH, D_MODEL), 0.02),
        "w_in": nrm(ks[2], (DEPTH, D_MODEL, IN_PROJ), D_MODEL ** -0.5),
        "conv_w": nrm(ks[3], (DEPTH, CONV_WIDTH, CONV_K), CONV_K ** -0.5),
        "gmlp_ln_g": 1.0 + nrm(ks[4], (DEPTH, GMLP_WIDTH), 0.02),
        "gmlp_ln_b": nrm(ks[5], (DEPTH, GMLP_WIDTH), 0.02),
        "gmlp_ws": nrm(ks[6], (DEPTH, GMLP_HEADS, GMLP_BLOCK, GMLP_BLOCK), GMLP_BLOCK ** -0.5),
        "gmlp_bs": 1.0 + nrm(ks[7], (DEPTH, GMLP_HEADS, GMLP_BLOCK), 0.02),
        "out_norm_conv_g": 1.0 + nrm(ks[8], (DEPTH, CONV_WIDTH), 0.02),
        "out_norm_gmlp_g": 1.0 + nrm(ks[9], (DEPTH, GMLP_WIDTH), 0.02),
        "w_out": nrm(ks[10], (DEPTH, MIX_WIDTH, D_MODEL), MIX_WIDTH ** -0.5),
        "norm2_g": 1.0 + nrm(ks[11], (DEPTH, D_MODEL), 0.02),
        "w_up": nrm(ks[12], (DEPTH, D_MODEL, D_FF), D_MODEL ** -0.5),
        "w_down": nrm(ks[13], (DEPTH, D_FF, D_MODEL), D_FF ** -0.5),
        "final_g": 1.0 + nrm(ks[14], (D_MODEL,), 0.02),
    }


def reference(x, norm1_g, w_in, conv_w, gmlp_ln_g, gmlp_ln_b, gmlp_ws, gmlp_bs,
              out_norm_conv_g, out_norm_gmlp_g, w_out, norm2_g, w_up, w_down, final_g):
    c = CONV_WIDTH
    for l in range(DEPTH):
        h = rmsnorm(x, norm1_g[l])
        p = jnp.einsum('bsd,de->bse', h, w_in[l])
        b_gate = p[..., 0:c]
        c_gate = p[..., c:2 * c]
        xc = p[..., 2 * c:3 * c]
        u = p[..., 3 * c:3 * c + GMLP_WIDTH]
        v = p[..., 3 * c + GMLP_WIDTH:]
        y_a = b_gate * causal_depthwise_conv(c_gate * xc, conv_w[l])
        v = layernorm(v, gmlp_ln_g[l], gmlp_ln_b[l])
        y_b = u * spatial_gating(v, gmlp_ws[l], gmlp_bs[l])
        y = jnp.concatenate([rmsnorm(y_a, out_norm_conv_g[l]),
                             rmsnorm(y_b, out_norm_gmlp_g[l])], axis=-1)
        x = x + jnp.einsum('bse,ed->bsd', y, w_out[l])
        h2 = rmsnorm(x, norm2_g[l])
        a = jax.nn.relu(jnp.einsum('bsd,df->bsf', h2, w_up[l]))
        x = x + jnp.einsum('bsf,fd->bsd', a * a, w_down[l])
    return rmsnorm(x, final_g)
```

```python
import functools

import jax
import jax.numpy as jnp
from jax import lax
from jax.experimental import pallas as pl
from jax.experimental.pallas import tpu as pltpu

EPS = 1e-6
CHUNK = 64
GMLP_BLOCK = 128
GMLP_HEADS = 8
CONV_K = 3
LANES = 128
SUBLANES = 8
HEADS_PER_VREG = 2

TILE = 512
FF_CHUNK = 1024
VMEM_LIMIT_BYTES = 56 * 1024 * 1024


def _rms(x, g):
    return x * lax.rsqrt(jnp.mean(x * x, axis=-1, keepdims=True) + EPS) * g


def _block_kernel(x_ref, g1_ref, w_in_ref, conv_w_ref, ln_g_ref, ln_b_ref, ws_ref, bias_ref,
                  gc_ref, gg_ref, w_out_ref, g2_ref, w_up_ref, w_down_ref, gf_ref,
                  o_ref, ws_pair_ref, zbuf_ref, yb_ref):
    f32, bf16 = jnp.float32, jnp.bfloat16
    tile = x_ref.shape[0]
    c = conv_w_ref.shape[1]
    gw = ln_g_ref.shape[1]
    d_ff = w_up_ref.shape[1]
    n_blocks = tile // GMLP_BLOCK
    n_pairs = GMLP_HEADS // HEADS_PER_VREG
    seq_step = pl.program_id(1)

    @pl.when((pl.program_id(0) == 0) & (seq_step == 0))
    def _():
        qi = lax.broadcasted_iota(jnp.int32, (GMLP_BLOCK, GMLP_BLOCK), 0) // CHUNK
        kj = lax.broadcasted_iota(jnp.int32, (GMLP_BLOCK, GMLP_BLOCK), 1) // CHUNK
        causal = kj <= qi
        for h in range(GMLP_HEADS):
            g, r = divmod(h, HEADS_PER_VREG)
            ws_pair_ref[g, :, r * GMLP_BLOCK:(r + 1) * GMLP_BLOCK] = (
                jnp.where(causal, ws_ref[h], 0.0).astype(bf16))

    @pl.when(seq_step == 0)
    def _():
        zbuf_ref[0:SUBLANES, :] = jnp.zeros((SUBLANES, c), f32)

    x = x_ref[...]
    h = _rms(x, g1_ref[...]).astype(bf16)

    def in_proj(lo, width):
        return jnp.dot(h, w_in_ref[:, lo:lo + width], preferred_element_type=f32)

    z = in_proj(c, c) * in_proj(2 * c, c)
    zbuf_ref[SUBLANES:SUBLANES + tile, :] = z
    cw = conv_w_ref[...]
    conv = (zbuf_ref[SUBLANES - 2:SUBLANES - 2 + tile, :] * cw[0:1, :]
            + zbuf_ref[SUBLANES - 1:SUBLANES - 1 + tile, :] * cw[1:2, :]
            + z * cw[2:3, :])
    zbuf_ref[0:SUBLANES, :] = zbuf_ref[tile:tile + SUBLANES, :]
    y_a = _rms(in_proj(0, c) * conv, gc_ref[...]).astype(bf16)

    v = in_proj(3 * c + gw, gw)
    mu = jnp.mean(v, axis=-1, keepdims=True)
    vc = v - mu
    vn = vc * lax.rsqrt(jnp.mean(vc * vc, axis=-1, keepdims=True) + EPS)
    vn = vn * ln_g_ref[...] + ln_b_ref[...]
    lane = lax.broadcasted_iota(jnp.int32, (tile, LANES), 1)
    low_head = lane < (LANES // HEADS_PER_VREG)
    for g in range(n_pairs):
        vg = vn[:, g * LANES:(g + 1) * LANES]
        v_lo = jnp.where(low_head, vg, 0.0).astype(bf16)
        v_hi = jnp.where(low_head, 0.0, vg).astype(bf16)
        rhs = jnp.concatenate(
            [jnp.concatenate([v_lo[n * GMLP_BLOCK:(n + 1) * GMLP_BLOCK, :] for n in range(n_blocks)], axis=1),
             jnp.concatenate([v_hi[n * GMLP_BLOCK:(n + 1) * GMLP_BLOCK, :] for n in range(n_blocks)], axis=1)],
            axis=0)
        mixed = jnp.dot(ws_pair_ref[g], rhs, preferred_element_type=f32)
        for n in range(n_blocks):
            yb_ref[n * GMLP_BLOCK:(n + 1) * GMLP_BLOCK, g * LANES:(g + 1) * LANES] = (
                mixed[:, n * LANES:(n + 1) * LANES])
    u = in_proj(3 * c, gw)
    bias = jnp.concatenate([bias_ref[...]] * n_blocks, axis=0)
    y_b = _rms(u * (yb_ref[...] + bias), gg_ref[...]).astype(bf16)

    y = jnp.concatenate([y_a, y_b], axis=-1)
    x1 = x + jnp.dot(y, w_out_ref[...], preferred_element_type=f32)

    h2 = _rms(x1, g2_ref[...]).astype(bf16)
    acc = x1
    for f in range(0, d_ff, FF_CHUNK):
        a = jnp.maximum(jnp.dot(h2, w_up_ref[:, f:f + FF_CHUNK], preferred_element_type=f32), 0.0)
        acc = acc + jnp.dot((a * a).astype(bf16), w_down_ref[f:f + FF_CHUNK, :],
                            preferred_element_type=f32)

    o_ref[...] = _rms(acc, gf_ref[...])


def _resident(shape):
    return pl.BlockSpec(shape, lambda b, s: (0,) * len(shape), pipeline_mode=pl.Buffered(1))


def kernel(x, norm1_g, w_in, conv_w, gmlp_ln_g, gmlp_ln_b, gmlp_ws, gmlp_bs, out_norm_conv_g,
           out_norm_gmlp_g, w_out, norm2_g, w_up, w_down, final_g):
    depth = w_in.shape[0]
    batch, seq, d_model = x.shape
    c = conv_w.shape[1]
    gw = gmlp_ln_g.shape[1]
    head_dim = gw // GMLP_HEADS
    assert depth == 1, "one fused block per call"
    assert seq % TILE == 0 and TILE % GMLP_BLOCK == 0
    assert head_dim * HEADS_PER_VREG == LANES
    assert w_up.shape[2] % FF_CHUNK == 0
    bf16 = jnp.bfloat16
    row = lambda a: a.reshape(1, -1)
    bias = jnp.repeat(jnp.transpose(gmlp_bs[0]), head_dim, axis=1)

    operands = (
        x,
        row(norm1_g[0]), w_in[0].astype(bf16), jnp.transpose(conv_w[0]),
        row(gmlp_ln_g[0]), row(gmlp_ln_b[0]), gmlp_ws[0], bias,
        row(out_norm_conv_g[0]), row(out_norm_gmlp_g[0]), w_out[0].astype(bf16),
        row(norm2_g[0]), w_up[0].astype(bf16), w_down[0].astype(bf16), row(final_g),
    )
    tok_spec = pl.BlockSpec((None, TILE, d_model), lambda b, s: (b, s, 0))
    in_specs = [tok_spec] + [_resident(a.shape) for a in operands[1:]]
    return pl.pallas_call(
        _block_kernel,
        grid=(batch, seq // TILE),
        in_specs=in_specs,
        out_specs=tok_spec,
        out_shape=jax.ShapeDtypeStruct(x.shape, x.dtype),
        scratch_shapes=[
            pltpu.VMEM((GMLP_HEADS // HEADS_PER_VREG, GMLP_BLOCK, HEADS_PER_VREG * GMLP_BLOCK), bf16),
            pltpu.VMEM((TILE + SUBLANES, c), jnp.float32),
            pltpu.VMEM((TILE, gw), jnp.float32),
        ],
        compiler_params=pltpu.CompilerParams(
            dimension_semantics=("arbitrary", "arbitrary"),
            vmem_limit_bytes=VMEM_LIMIT_BYTES),
        name="hybrid_block",
    )(*operands)
```

```python
import functools

import jax
import jax.numpy as jnp
from jax import lax
from jax.experimental import pallas as pl
from jax.experimental.pallas import tpu as pltpu

EPS = 1e-6
CHUNK = 64
GMLP_BLOCK = 128
GMLP_HEADS = 8
LANES = 128
SUBLANES = 8
HEADS_PER_VREG = 2

TILE = 512
N_FF_CHUNKS = 4
VMEM_LIMIT_BYTES = 56 * 1024 * 1024


def _rms(x, g):
    return x * lax.rsqrt(jnp.mean(x * x, axis=-1, keepdims=True) + EPS) * g


def _block_kernel(x_ref, g1_ref, w_in_ref, conv_w_ref, ln_g_ref, ln_b_ref, ws_ref, bias_ref,
                  gc_ref, gg_ref, w_out_ref, g2_ref, w_up_ref, w_down_ref, gf_ref,
                  o_ref, ws_pair_ref, zbuf_ref, yb_ref, x1_ref, h2_ref, pre_ref,
                  *, n_tiles, tiles_per_batch):
    f32, bf16 = jnp.float32, jnp.bfloat16
    tile = x_ref.shape[0]
    c = conv_w_ref.shape[1]
    gw = ln_g_ref.shape[1]
    ff_chunk = w_up_ref.shape[1] // N_FF_CHUNKS
    n_blocks = tile // GMLP_BLOCK
    n_pairs = GMLP_HEADS // HEADS_PER_VREG
    step = pl.program_id(0)

    def prepare():
        qi = lax.broadcasted_iota(jnp.int32, (GMLP_BLOCK, GMLP_BLOCK), 0) // CHUNK
        kj = lax.broadcasted_iota(jnp.int32, (GMLP_BLOCK, GMLP_BLOCK), 1) // CHUNK
        causal = kj <= qi
        for h in range(GMLP_HEADS):
            g, r = divmod(h, HEADS_PER_VREG)
            ws_pair_ref[g, :, r * GMLP_BLOCK:(r + 1) * GMLP_BLOCK] = (
                jnp.where(causal, ws_ref[h], 0.0).astype(bf16))
        zbuf_ref[0:SUBLANES, :] = jnp.zeros((SUBLANES, c), f32)
        pre_ref[...] = jnp.zeros(pre_ref.shape, f32)

    def stage_a():
        x = x_ref[...]
        h = _rms(x, g1_ref[...]).astype(bf16)
        yield

        def in_proj(lo, width):
            return jnp.dot(h, w_in_ref[:, lo:lo + width], preferred_element_type=f32)

        z = in_proj(c, c) * in_proj(2 * c, c)
        zbuf_ref[SUBLANES:SUBLANES + tile, :] = z
        cw = conv_w_ref[...]
        conv = (zbuf_ref[SUBLANES - 2:SUBLANES - 2 + tile, :] * cw[0:1, :]
                + zbuf_ref[SUBLANES - 1:SUBLANES - 1 + tile, :] * cw[1:2, :]
                + z * cw[2:3, :])
        next_starts_sequence = (step + 1) % tiles_per_batch == 0
        zbuf_ref[0:SUBLANES, :] = jnp.where(next_starts_sequence, 0.0,
                                            zbuf_ref[tile:tile + SUBLANES, :])
        yield
        y_a = _rms(in_proj(0, c) * conv, gc_ref[...]).astype(bf16)
        yield

        v = in_proj(3 * c + gw, gw)
        mu = jnp.mean(v, axis=-1, keepdims=True)
        vc = v - mu
        vn = vc * lax.rsqrt(jnp.mean(vc * vc, axis=-1, keepdims=True) + EPS)
        vn = vn * ln_g_ref[...] + ln_b_ref[...]
        yield
        lane = lax.broadcasted_iota(jnp.int32, (tile, LANES), 1)
        low_head = lane < (LANES // HEADS_PER_VREG)
        for g in range(n_pairs):
            vg = vn[:, g * LANES:(g + 1) * LANES]
            v_lo = jnp.where(low_head, vg, 0.0).astype(bf16)
            v_hi = jnp.where(low_head, 0.0, vg).astype(bf16)
            rhs = jnp.concatenate(
                [jnp.concatenate([v_lo[n * GMLP_BLOCK:(n + 1) * GMLP_BLOCK, :]
                                  for n in range(n_blocks)], axis=1),
                 jnp.concatenate([v_hi[n * GMLP_BLOCK:(n + 1) * GMLP_BLOCK, :]
                                  for n in range(n_blocks)], axis=1)],
                axis=0)
            mixed = jnp.dot(ws_pair_ref[g], rhs, preferred_element_type=f32)
            for n in range(n_blocks):
                yb_ref[n * GMLP_BLOCK:(n + 1) * GMLP_BLOCK, g * LANES:(g + 1) * LANES] = (
                    mixed[:, n * LANES:(n + 1) * LANES])
        yield
        u = in_proj(3 * c, gw)
        bias = jnp.concatenate([bias_ref[...]] * n_blocks, axis=0)
        y_b = _rms(u * (yb_ref[...] + bias), gg_ref[...]).astype(bf16)
        yield

        y = jnp.concatenate([y_a, y_b], axis=-1)
        x1_ref[...] = x + jnp.dot(y, w_out_ref[...], preferred_element_type=f32)
        yield
        h2_ref[...] = _rms(x1_ref[...], g2_ref[...]).astype(bf16)
        yield

    def stage_b():
        acc = None
        for k in range(N_FF_CHUNKS):
            cols = slice(k * ff_chunk, (k + 1) * ff_chunk)
            a = jnp.maximum(
                jnp.dot(h2_ref[...], w_up_ref[:, cols], preferred_element_type=f32), 0.0)
            a = (a * a).astype(bf16)
            yield
            if acc is None:
                acc = x1_ref[...]
            acc = acc + jnp.dot(a, w_down_ref[cols, :], preferred_element_type=f32)
            if k == N_FF_CHUNKS - 1:
                pre_ref[...] = acc
            yield

    def stage_c():
        o_ref[...] = _rms(pre_ref[...], gf_ref[...])

    def run(order):
        gens = {}
        for name in order:
            if name == "c":
                stage_c()
                continue
            if name not in gens:
                gens[name] = {"a": stage_a, "b": stage_b}[name]()
            next(gens[name])
        for gen in gens.values():
            assert next(gen, "done") == "done"

    @pl.when(step == 0)
    def _():
        prepare()
        run("aaaaaaaa")

    @pl.when((step > 0) & (step < n_tiles))
    def _():
        run("bc" "a" "aba" "ba" "bb" "aa" "bb" "aba")

    @pl.when(step == n_tiles)
    def _():
        run("bc" "bbbbbbb")

    @pl.when(step == n_tiles + 1)
    def _():
        run("c")


def _resident(shape):
    return pl.BlockSpec(shape, lambda i: (0,) * len(shape), pipeline_mode=pl.Buffered(1))


def kernel(x, norm1_g, w_in, conv_w, gmlp_ln_g, gmlp_ln_b, gmlp_ws, gmlp_bs, out_norm_conv_g,
           out_norm_gmlp_g, w_out, norm2_g, w_up, w_down, final_g):
    depth = w_in.shape[0]
    batch, seq, d_model = x.shape
    c = conv_w.shape[1]
    gw = gmlp_ln_g.shape[1]
    head_dim = gw // GMLP_HEADS
    assert depth == 1, "one fused block per call"
    assert seq % TILE == 0 and TILE % GMLP_BLOCK == 0
    assert head_dim * HEADS_PER_VREG == LANES
    assert w_up.shape[2] % N_FF_CHUNKS == 0
    tiles_per_batch = seq // TILE
    n_tiles = batch * tiles_per_batch
    bf16 = jnp.bfloat16
    row = lambda a: a.reshape(1, -1)
    bias = jnp.repeat(jnp.transpose(gmlp_bs[0]), head_dim, axis=1)

    operands = (
        x.reshape(batch * seq, d_model),
        row(norm1_g[0]), w_in[0].astype(bf16), jnp.transpose(conv_w[0]),
        row(gmlp_ln_g[0]), row(gmlp_ln_b[0]), gmlp_ws[0], bias,
        row(out_norm_conv_g[0]), row(out_norm_gmlp_g[0]), w_out[0].astype(bf16),
        row(norm2_g[0]), w_up[0].astype(bf16), w_down[0].astype(bf16), row(final_g),
    )
    x_spec = pl.BlockSpec((TILE, d_model), lambda i: (jnp.minimum(i, n_tiles - 1), 0))
    o_spec = pl.BlockSpec((TILE, d_model), lambda i: (jnp.maximum(i - 2, 0), 0))
    out = pl.pallas_call(
        functools.partial(_block_kernel, n_tiles=n_tiles, tiles_per_batch=tiles_per_batch),
        grid=(n_tiles + 2,),
        in_specs=[x_spec] + [_resident(a.shape) for a in operands[1:]],
        out_specs=o_spec,
        out_shape=jax.ShapeDtypeStruct((batch * seq, d_model), x.dtype),
        scratch_shapes=[
            pltpu.VMEM((GMLP_HEADS // HEADS_PER_VREG, GMLP_BLOCK, HEADS_PER_VREG * GMLP_BLOCK), bf16),
            pltpu.VMEM((TILE + SUBLANES, c), jnp.float32),
            pltpu.VMEM((TILE, gw), jnp.float32),
            pltpu.VMEM((TILE, d_model), jnp.float32),
            pltpu.VMEM((TILE, d_model), bf16),
            pltpu.VMEM((TILE, d_model), jnp.float32),
        ],
        compiler_params=pltpu.CompilerParams(
            dimension_semantics=("arbitrary",),
            vmem_limit_bytes=VMEM_LIMIT_BYTES),
        name="hybrid_block",
    )(*operands)
    return out.reshape(batch, seq, d_model)
```

```python
import functools

import jax
import jax.numpy as jnp
from jax import lax
from jax.experimental import pallas as pl
from jax.experimental.pallas import tpu as pltpu

EPS = 1e-6
CHUNK = 64
GMLP_BLOCK = 128
GMLP_HEADS = 8
LANES = 128
SUBLANES = 8
HEADS_PER_VREG = 2

TILE = 512
N_FF_CHUNKS = 4
VMEM_LIMIT_BYTES = 56 * 1024 * 1024


def _inv_rms(x):
    return lax.rsqrt(jnp.mean(x * x, axis=-1, keepdims=True) + EPS)


def _rms(x, g):
    return x * _inv_rms(x) * g


def _block_kernel(x_ref, g1_ref, w_in_ref, conv_w_ref, ln_g_ref, ln_b_ref, ws_ref, bias_ref,
                  gc_ref, gg_ref, w_out_ref, g2_ref, w_up_ref, w_down_ref, gf_ref,
                  o_ref, ws_pair_ref, zbuf_ref, yb_ref, x1_ref, h2_ref, rr2_ref, pre_ref, ssq_ref,
                  *, n_tiles, tiles_per_batch):
    f32, bf16 = jnp.float32, jnp.bfloat16
    tile = x_ref.shape[0]
    c = conv_w_ref.shape[1]
    gw = ln_g_ref.shape[1]
    ff_chunk = w_up_ref.shape[1] // N_FF_CHUNKS
    n_blocks = tile // GMLP_BLOCK
    n_pairs = GMLP_HEADS // HEADS_PER_VREG
    step = pl.program_id(0)

    def prepare():
        qi = lax.broadcasted_iota(jnp.int32, (GMLP_BLOCK, GMLP_BLOCK), 0) // CHUNK
        kj = lax.broadcasted_iota(jnp.int32, (GMLP_BLOCK, GMLP_BLOCK), 1) // CHUNK
        causal = kj <= qi
        for h in range(GMLP_HEADS):
            g, r = divmod(h, HEADS_PER_VREG)
            ws_pair_ref[g, :, r * GMLP_BLOCK:(r + 1) * GMLP_BLOCK] = (
                jnp.where(causal, ws_ref[h], 0.0).astype(bf16))
        zbuf_ref[0:SUBLANES, :] = jnp.zeros((SUBLANES, c), f32)
        pre_ref[...] = jnp.zeros(pre_ref.shape, f32)
        ssq_ref[...] = jnp.zeros(ssq_ref.shape, f32)

    def stage_a():
        x = x_ref[...]
        h = _rms(x, g1_ref[...]).astype(bf16)
        yield

        def in_proj(lo, width):
            return jnp.dot(h, w_in_ref[:, lo:lo + width], preferred_element_type=f32)

        z = in_proj(c, c) * in_proj(2 * c, c)
        zbuf_ref[SUBLANES:SUBLANES + tile, :] = z
        cw = conv_w_ref[...]
        conv = (zbuf_ref[SUBLANES - 2:SUBLANES - 2 + tile, :] * cw[0:1, :]
                + zbuf_ref[SUBLANES - 1:SUBLANES - 1 + tile, :] * cw[1:2, :]
                + z * cw[2:3, :])
        next_starts_sequence = (step + 1) % tiles_per_batch == 0
        zbuf_ref[0:SUBLANES, :] = jnp.where(next_starts_sequence, 0.0,
                                            zbuf_ref[tile:tile + SUBLANES, :])
        yield
        y_a = _rms(in_proj(0, c) * conv, gc_ref[...]).astype(bf16)
        yield

        v = in_proj(3 * c + gw, gw)
        mu = jnp.mean(v, axis=-1, keepdims=True)
        vc = v - mu
        vn = vc * lax.rsqrt(jnp.mean(vc * vc, axis=-1, keepdims=True) + EPS)
        vn = vn * ln_g_ref[...] + ln_b_ref[...]
        yield
        lane = lax.broadcasted_iota(jnp.int32, (tile, LANES), 1)
        low_head = lane < (LANES // HEADS_PER_VREG)
        for g in range(n_pairs):
            vg = vn[:, g * LANES:(g + 1) * LANES]
            v_lo = jnp.where(low_head, vg, 0.0).astype(bf16)
            v_hi = jnp.where(low_head, 0.0, vg).astype(bf16)
            rhs = jnp.concatenate(
                [jnp.concatenate([v_lo[n * GMLP_BLOCK:(n + 1) * GMLP_BLOCK, :]
                                  for n in range(n_blocks)], axis=1),
                 jnp.concatenate([v_hi[n * GMLP_BLOCK:(n + 1) * GMLP_BLOCK, :]
                                  for n in range(n_blocks)], axis=1)],
                axis=0)
            mixed = jnp.dot(ws_pair_ref[g], rhs, preferred_element_type=f32)
            for n in range(n_blocks):
                yb_ref[n * GMLP_BLOCK:(n + 1) * GMLP_BLOCK, g * LANES:(g + 1) * LANES] = (
                    mixed[:, n * LANES:(n + 1) * LANES])
        yield
        u = in_proj(3 * c, gw)
        bias = jnp.concatenate([bias_ref[...]] * n_blocks, axis=0)
        y_b = _rms(u * (yb_ref[...] + bias), gg_ref[...]).astype(bf16)
        yield

        y = jnp.concatenate([y_a, y_b], axis=-1)
        x1 = x + jnp.dot(y, w_out_ref[...], preferred_element_type=f32)
        x1_ref[...] = x1
        h2_ref[...] = (x1 * g2_ref[...]).astype(bf16)
        r2 = _inv_rms(x1)
        rr2_ref[...] = jnp.broadcast_to(r2 * r2, rr2_ref.shape)
        yield

    def stage_b():
        rr2 = jnp.concatenate([rr2_ref[...]] * (ff_chunk // LANES), axis=1)
        acc = None
        for k in range(N_FF_CHUNKS):
            cols = slice(k * ff_chunk, (k + 1) * ff_chunk)
            a = jnp.maximum(
                jnp.dot(h2_ref[...], w_up_ref[:, cols], preferred_element_type=f32), 0.0)
            a = (a * a * rr2).astype(bf16)
            yield
            if acc is None:
                acc = x1_ref[...]
            acc = acc + jnp.dot(a, w_down_ref[cols, :], preferred_element_type=f32)
            if k == N_FF_CHUNKS - 1:
                pre_ref[...] = acc
                sq = acc * acc
                ssq_ref[...] = sum(sq[:, j:j + LANES] for j in range(0, sq.shape[1], LANES))
            yield

    def stage_c():
        pre = pre_ref[...]
        ms = jnp.sum(ssq_ref[...], axis=-1, keepdims=True) * (1.0 / pre.shape[1])
        o_ref[...] = pre * lax.rsqrt(ms + EPS) * gf_ref[...]

    def run(order):
        gens = {}
        for name in order:
            if name == "c":
                stage_c()
                continue
            if name not in gens:
                gens[name] = {"a": stage_a, "b": stage_b}[name]()
            next(gens[name])
        for gen in gens.values():
            assert next(gen, "done") == "done"

    @pl.when(step == 0)
    def _():
        prepare()
        run("aaaaaaa")

    @pl.when((step > 0) & (step < n_tiles))
    def _():
        run("bc" "a" "aba" "ba" "bb" "aa" "bb" "ab")

    @pl.when(step == n_tiles)
    def _():
        run("bc" "bbbbbbb")

    @pl.when(step == n_tiles + 1)
    def _():
        run("c")


def _resident(shape):
    return pl.BlockSpec(shape, lambda i: (0,) * len(shape), pipeline_mode=pl.Buffered(1))


def kernel(x, norm1_g, w_in, conv_w, gmlp_ln_g, gmlp_ln_b, gmlp_ws, gmlp_bs, out_norm_conv_g,
           out_norm_gmlp_g, w_out, norm2_g, w_up, w_down, final_g):
    depth = w_in.shape[0]
    batch, seq, d_model = x.shape
    c = conv_w.shape[1]
    gw = gmlp_ln_g.shape[1]
    head_dim = gw // GMLP_HEADS
    assert depth == 1, "one fused block per call"
    assert seq % TILE == 0 and TILE % GMLP_BLOCK == 0
    assert head_dim * HEADS_PER_VREG == LANES
    assert w_up.shape[2] % N_FF_CHUNKS == 0
    tiles_per_batch = seq // TILE
    n_tiles = batch * tiles_per_batch
    bf16 = jnp.bfloat16
    row = lambda a: a.reshape(1, -1)
    bias = jnp.repeat(jnp.transpose(gmlp_bs[0]), head_dim, axis=1)

    operands = (
        x.reshape(batch * seq, d_model),
        row(norm1_g[0]), w_in[0].astype(bf16), jnp.transpose(conv_w[0]),
        row(gmlp_ln_g[0]), row(gmlp_ln_b[0]), gmlp_ws[0], bias,
        row(out_norm_conv_g[0]), row(out_norm_gmlp_g[0]), w_out[0].astype(bf16),
        row(norm2_g[0]), w_up[0].astype(bf16), w_down[0].astype(bf16), row(final_g),
    )
    x_spec = pl.BlockSpec((TILE, d_model), lambda i: (jnp.minimum(i, n_tiles - 1), 0))
    o_spec = pl.BlockSpec((TILE, d_model), lambda i: (jnp.maximum(i - 2, 0), 0))
    out = pl.pallas_call(
        functools.partial(_block_kernel, n_tiles=n_tiles, tiles_per_batch=tiles_per_batch),
        grid=(n_tiles + 2,),
        in_specs=[x_spec] + [_resident(a.shape) for a in operands[1:]],
        out_specs=o_spec,
        out_shape=jax.ShapeDtypeStruct((batch * seq, d_model), x.dtype),
        scratch_shapes=[
            pltpu.VMEM((GMLP_HEADS // HEADS_PER_VREG, GMLP_BLOCK, HEADS_PER_VREG * GMLP_BLOCK), bf16),
            pltpu.VMEM((TILE + SUBLANES, c), jnp.float32),
            pltpu.VMEM((TILE, gw), jnp.float32),
            pltpu.VMEM((TILE, d_model), jnp.float32),
            pltpu.VMEM((TILE, d_model), bf16),
            pltpu.VMEM((TILE, LANES), jnp.float32),
            pltpu.VMEM((TILE, d_model), jnp.float32),
            pltpu.VMEM((TILE, LANES), jnp.float32),
        ],
        compiler_params=pltpu.CompilerParams(
            dimension_semantics=("arbitrary",),
            vmem_limit_bytes=VMEM_LIMIT_BYTES),
        name="hybrid_block",
    )(*operands)
    return out.reshape(batch, seq, d_model)
```

```python
import functools

import jax
import jax.numpy as jnp
from jax import lax
from jax.experimental import pallas as pl
from jax.experimental.pallas import tpu as pltpu

EPS = 1e-6
CHUNK = 64
GMLP_BLOCK = 128
GMLP_HEADS = 8
LANES = 128
SUBLANES = 8
HEADS_PER_VREG = 2

TILE = 512
TILES_PER_STEP = 2
N_FF_CHUNKS = 4
VMEM_LIMIT_BYTES = 58 * 1024 * 1024


def _inv_rms(x):
    return lax.rsqrt(jnp.mean(x * x, axis=-1, keepdims=True) + EPS)


def _rms(x, g):
    return x * _inv_rms(x) * g


def _block_kernel(x_ref, g1_ref, w_in_ref, conv_w_ref, ln_g_ref, ln_b_ref, ws_ref, bias_ref,
                  gc_ref, gg_ref, w_out_ref, g2_ref, w_up_ref, w_down_ref, gf_ref,
                  o_ref, ws_pair_ref, zbuf_ref, yb_ref, x1_ref, h2_ref, rr2_ref, pre_ref, ssq_ref,
                  *, n_steps, tiles_per_batch):
    f32, bf16 = jnp.float32, jnp.bfloat16
    tile = TILE
    c = conv_w_ref.shape[1]
    gw = ln_g_ref.shape[1]
    ff_chunk = w_up_ref.shape[1] // N_FF_CHUNKS
    n_blocks = tile // GMLP_BLOCK
    n_pairs = GMLP_HEADS // HEADS_PER_VREG
    step = pl.program_id(0)

    def prepare():
        qi = lax.broadcasted_iota(jnp.int32, (GMLP_BLOCK, GMLP_BLOCK), 0) // CHUNK
        kj = lax.broadcasted_iota(jnp.int32, (GMLP_BLOCK, GMLP_BLOCK), 1) // CHUNK
        causal = kj <= qi
        for h in range(GMLP_HEADS):
            g, r = divmod(h, HEADS_PER_VREG)
            ws_pair_ref[g, :, r * GMLP_BLOCK:(r + 1) * GMLP_BLOCK] = (
                jnp.where(causal, ws_ref[h], 0.0).astype(bf16))
        zbuf_ref[0:SUBLANES, :] = jnp.zeros((SUBLANES, c), f32)
        pre_ref[...] = jnp.zeros(pre_ref.shape, f32)
        ssq_ref[...] = jnp.zeros(ssq_ref.shape, f32)

    def stage_a(half):
        x = x_ref[half * tile:(half + 1) * tile, :]
        h = _rms(x, g1_ref[...]).astype(bf16)
        yield

        def in_proj(lo, width):
            return jnp.dot(h, w_in_ref[:, lo:lo + width], preferred_element_type=f32)

        z = in_proj(c, c) * in_proj(2 * c, c)
        zbuf_ref[SUBLANES:SUBLANES + tile, :] = z
        cw = conv_w_ref[...]
        conv = (zbuf_ref[SUBLANES - 2:SUBLANES - 2 + tile, :] * cw[0:1, :]
                + zbuf_ref[SUBLANES - 1:SUBLANES - 1 + tile, :] * cw[1:2, :]
                + z * cw[2:3, :])
        next_starts_sequence = (step * TILES_PER_STEP + half + 1) % tiles_per_batch == 0
        zbuf_ref[0:SUBLANES, :] = jnp.where(next_starts_sequence, 0.0,
                                            zbuf_ref[tile:tile + SUBLANES, :])
        yield
        y_a = _rms(in_proj(0, c) * conv, gc_ref[...]).astype(bf16)
        yield

        v = in_proj(3 * c + gw, gw)
        mu = jnp.mean(v, axis=-1, keepdims=True)
        vc = v - mu
        vn = vc * lax.rsqrt(jnp.mean(vc * vc, axis=-1, keepdims=True) + EPS)
        vn = vn * ln_g_ref[...] + ln_b_ref[...]
        yield
        lane = lax.broadcasted_iota(jnp.int32, (tile, LANES), 1)
        low_head = lane < (LANES // HEADS_PER_VREG)
        for g in range(n_pairs):
            vg = vn[:, g * LANES:(g + 1) * LANES]
            v_lo = jnp.where(low_head, vg, 0.0).astype(bf16)
            v_hi = jnp.where(low_head, 0.0, vg).astype(bf16)
            rhs = jnp.concatenate(
                [jnp.concatenate([v_lo[n * GMLP_BLOCK:(n + 1) * GMLP_BLOCK, :]
                                  for n in range(n_blocks)], axis=1),
                 jnp.concatenate([v_hi[n * GMLP_BLOCK:(n + 1) * GMLP_BLOCK, :]
                                  for n in range(n_blocks)], axis=1)],
                axis=0)
            mixed = jnp.dot(ws_pair_ref[g], rhs, preferred_element_type=f32)
            for n in range(n_blocks):
                yb_ref[n * GMLP_BLOCK:(n + 1) * GMLP_BLOCK, g * LANES:(g + 1) * LANES] = (
                    mixed[:, n * LANES:(n + 1) * LANES])
        yield
        u = in_proj(3 * c, gw)
        bias = jnp.concatenate([bias_ref[...]] * n_blocks, axis=0)
        y_b = _rms(u * (yb_ref[...] + bias), gg_ref[...]).astype(bf16)
        yield

        y = jnp.concatenate([y_a, y_b], axis=-1)
        x1 = x + jnp.dot(y, w_out_ref[...], preferred_element_type=f32)
        x1_ref[...] = x1
        h2_ref[...] = (x1 * g2_ref[...]).astype(bf16)
        r2 = _inv_rms(x1)
        rr2_ref[...] = jnp.broadcast_to(r2 * r2, rr2_ref.shape)
        yield

    def stage_b():
        rr2 = jnp.concatenate([rr2_ref[...]] * (ff_chunk // LANES), axis=1)
        acc = None
        for k in range(N_FF_CHUNKS):
            cols = slice(k * ff_chunk, (k + 1) * ff_chunk)
            a = jnp.maximum(
                jnp.dot(h2_ref[...], w_up_ref[:, cols], preferred_element_type=f32), 0.0)
            a = (a * a * rr2).astype(bf16)
            yield
            if acc is None:
                acc = x1_ref[...]
            acc = acc + jnp.dot(a, w_down_ref[cols, :], preferred_element_type=f32)
            if k == N_FF_CHUNKS - 1:
                pre_ref[...] = acc
                sq = acc * acc
                ssq_ref[...] = sum(sq[:, j:j + LANES] for j in range(0, sq.shape[1], LANES))
            yield

    def stage_c(half):
        pre = pre_ref[...]
        ms = jnp.sum(ssq_ref[...], axis=-1, keepdims=True) * (1.0 / pre.shape[1])
        o_ref[half * tile:(half + 1) * tile, :] = pre * lax.rsqrt(ms + EPS) * gf_ref[...]

    def run(order, half):
        gens = {}
        for name in order:
            if name == "c":
                stage_c(half)
                continue
            if name not in gens:
                gens[name] = stage_a(half) if name == "a" else stage_b()
            next(gens[name])
        for gen in gens.values():
            assert next(gen, "done") == "done"

    only_a, only_c = "aaaaaaa", "c"
    b_and_c = "bc" "bbbbbbb"
    full = "bc" "a" "aba" "ba" "bb" "aa" "bb" "ab"
    assert TILES_PER_STEP == 2

    @pl.when(step == 0)
    def _():
        prepare()
        run(only_a, 0)
        run(full, 1)

    @pl.when((step > 0) & (step < n_steps))
    def _():
        run(full, 0)
        run(full, 1)

    @pl.when(step == n_steps)
    def _():
        run(b_and_c, 0)
        run(only_c, 1)


def _resident(shape):
    return pl.BlockSpec(shape, lambda i: (0,) * len(shape), pipeline_mode=pl.Buffered(1))


def kernel(x, norm1_g, w_in, conv_w, gmlp_ln_g, gmlp_ln_b, gmlp_ws, gmlp_bs, out_norm_conv_g,
           out_norm_gmlp_g, w_out, norm2_g, w_up, w_down, final_g):
    depth = w_in.shape[0]
    batch, seq, d_model = x.shape
    c = conv_w.shape[1]
    gw = gmlp_ln_g.shape[1]
    head_dim = gw // GMLP_HEADS
    assert depth == 1, "one fused block per call"
    assert seq % TILE == 0 and TILE % GMLP_BLOCK == 0
    assert (batch * seq) % (TILES_PER_STEP * TILE) == 0
    assert head_dim * HEADS_PER_VREG == LANES
    assert w_up.shape[2] % N_FF_CHUNKS == 0
    tiles_per_batch = seq // TILE
    n_steps = batch * tiles_per_batch // TILES_PER_STEP
    block_rows = TILES_PER_STEP * TILE
    bf16 = jnp.bfloat16
    row = lambda a: a.reshape(1, -1)
    bias = jnp.repeat(jnp.transpose(gmlp_bs[0]), head_dim, axis=1)

    operands = (
        x.reshape(batch * seq, d_model),
        row(norm1_g[0]), w_in[0].astype(bf16), jnp.transpose(conv_w[0]),
        row(gmlp_ln_g[0]), row(gmlp_ln_b[0]), gmlp_ws[0], bias,
        row(out_norm_conv_g[0]), row(out_norm_gmlp_g[0]), w_out[0].astype(bf16),
        row(norm2_g[0]), w_up[0].astype(bf16), w_down[0].astype(bf16), row(final_g),
    )
    x_spec = pl.BlockSpec((block_rows, d_model), lambda i: (jnp.minimum(i, n_steps - 1), 0))
    o_spec = pl.BlockSpec((block_rows, d_model), lambda i: (jnp.maximum(i - 1, 0), 0))
    out = pl.pallas_call(
        functools.partial(_block_kernel, n_steps=n_steps, tiles_per_batch=tiles_per_batch),
        grid=(n_steps + 1,),
        in_specs=[x_spec] + [_resident(a.shape) for a in operands[1:]],
        out_specs=o_spec,
        out_shape=jax.ShapeDtypeStruct((batch * seq, d_model), x.dtype),
        scratch_shapes=[
            pltpu.VMEM((GMLP_HEADS // HEADS_PER_VREG, GMLP_BLOCK, HEADS_PER_VREG * GMLP_BLOCK), bf16),
            pltpu.VMEM((TILE + SUBLANES, c), jnp.float32),
            pltpu.VMEM((TILE, gw), jnp.float32),
            pltpu.VMEM((TILE, d_model), jnp.float32),
            pltpu.VMEM((TILE, d_model), bf16),
            pltpu.VMEM((TILE, LANES), jnp.float32),
            pltpu.VMEM((TILE, d_model), jnp.float32),
            pltpu.VMEM((TILE, LANES), jnp.float32),
        ],
        compiler_params=pltpu.CompilerParams(
            dimension_semantics=("arbitrary",),
            vmem_limit_bytes=VMEM_LIMIT_BYTES),
        name="hybrid_block",
    )(*operands)
    return out.reshape(batch, seq, d_model)
```

```python
import functools

import jax
import jax.numpy as jnp
from jax import lax
from jax.experimental import pallas as pl
from jax.experimental.pallas import tpu as pltpu

EPS = 1e-6
CHUNK = 64
GMLP_BLOCK = 128
GMLP_HEADS = 8
LANES = 128
SUBLANES = 8
HEADS_PER_VREG = 2

TILE = 512
N_FF_CHUNKS = 4
VMEM_LIMIT_BYTES = 56 * 1024 * 1024

WEIGHT_CHUNK_BYTES = 1024 * 1024
WEIGHT_LOAD_SLOTS = 4


def _inv_rms(x):
    return lax.rsqrt(jnp.mean(x * x, axis=-1, keepdims=True) + EPS)


def _rms(x, g):
    return x * _inv_rms(x) * g


def _load_weight_as_bf16(src_hbm, dst_ref):
    n_rows, n_cols = src_hbm.shape
    rows = 1 << ((WEIGHT_CHUNK_BYTES // (n_cols * 4)).bit_length() - 1)
    n_chunks = n_rows // rows
    assert rows * n_chunks == n_rows and n_chunks >= WEIGHT_LOAD_SLOTS

    def body(stage_ref, sem_ref):
        def chunk_copy(i, slot):
            return pltpu.make_async_copy(
                src_hbm.at[pl.ds(i * rows, rows), :], stage_ref.at[slot], sem_ref.at[slot])

        for i in range(WEIGHT_LOAD_SLOTS - 1):
            chunk_copy(i, i).start()

        def step(i, carry):
            slot = i % WEIGHT_LOAD_SLOTS
            ahead = i + WEIGHT_LOAD_SLOTS - 1

            @pl.when(ahead < n_chunks)
            def _():
                chunk_copy(ahead, ahead % WEIGHT_LOAD_SLOTS).start()

            chunk_copy(i, slot).wait()
            dst_ref[pl.ds(pl.multiple_of(i * rows, rows), rows), :] = (
                stage_ref[slot].astype(jnp.bfloat16))
            return carry

        lax.fori_loop(0, n_chunks, step, 0)

    pl.run_scoped(body,
                  pltpu.VMEM((WEIGHT_LOAD_SLOTS, rows, n_cols), jnp.float32),
                  pltpu.SemaphoreType.DMA((WEIGHT_LOAD_SLOTS,)))


def _block_kernel(x_ref, g1_ref, w_in_hbm, conv_w_ref, ln_g_ref, ln_b_ref, ws_ref, bias_ref,
                  gc_ref, gg_ref, w_out_hbm, g2_ref, w_up_hbm, w_down_hbm, gf_ref,
                  o_ref, w_in_ref, w_out_ref, w_up_ref, w_down_ref,
                  ws_pair_ref, zbuf_ref, yb_ref, x1_ref, h2_ref, rr2_ref, pre_ref, ssq_ref,
                  *, n_tiles, tiles_per_batch):
    f32, bf16 = jnp.float32, jnp.bfloat16
    tile = x_ref.shape[0]
    c = conv_w_ref.shape[1]
    gw = ln_g_ref.shape[1]
    ff_chunk = w_up_ref.shape[1] // N_FF_CHUNKS
    n_blocks = tile // GMLP_BLOCK
    n_pairs = GMLP_HEADS // HEADS_PER_VREG
    step = pl.program_id(0)

    def prepare():
        _load_weight_as_bf16(w_in_hbm, w_in_ref)
        _load_weight_as_bf16(w_out_hbm, w_out_ref)
        _load_weight_as_bf16(w_up_hbm, w_up_ref)
        _load_weight_as_bf16(w_down_hbm, w_down_ref)
        qi = lax.broadcasted_iota(jnp.int32, (GMLP_BLOCK, GMLP_BLOCK), 0) // CHUNK
        kj = lax.broadcasted_iota(jnp.int32, (GMLP_BLOCK, GMLP_BLOCK), 1) // CHUNK
        causal = kj <= qi
        for h in range(GMLP_HEADS):
            g, r = divmod(h, HEADS_PER_VREG)
            ws_pair_ref[g, :, r * GMLP_BLOCK:(r + 1) * GMLP_BLOCK] = (
                jnp.where(causal, ws_ref[h], 0.0).astype(bf16))
        zbuf_ref[0:SUBLANES, :] = jnp.zeros((SUBLANES, c), f32)
        pre_ref[...] = jnp.zeros(pre_ref.shape, f32)
        ssq_ref[...] = jnp.zeros(ssq_ref.shape, f32)

    def stage_a():
        x = x_ref[...]
        h = _rms(x, g1_ref[...]).astype(bf16)
        yield

        def in_proj(lo, width):
            return jnp.dot(h, w_in_ref[:, lo:lo + width], preferred_element_type=f32)

        z = in_proj(c, c) * in_proj(2 * c, c)
        zbuf_ref[SUBLANES:SUBLANES + tile, :] = z
        cw = conv_w_ref[...]
        conv = (zbuf_ref[SUBLANES - 2:SUBLANES - 2 + tile, :] * cw[0:1, :]
                + zbuf_ref[SUBLANES - 1:SUBLANES - 1 + tile, :] * cw[1:2, :]
                + z * cw[2:3, :])
        next_starts_sequence = (step + 1) % tiles_per_batch == 0
        zbuf_ref[0:SUBLANES, :] = jnp.where(next_starts_sequence, 0.0,
                                            zbuf_ref[tile:tile + SUBLANES, :])
        yield
        y_a = _rms(in_proj(0, c) * conv, gc_ref[...]).astype(bf16)
        yield

        v = in_proj(3 * c + gw, gw)
        mu = jnp.mean(v, axis=-1, keepdims=True)
        vc = v - mu
        vn = vc * lax.rsqrt(jnp.mean(vc * vc, axis=-1, keepdims=True) + EPS)
        vn = vn * ln_g_ref[...] + ln_b_ref[...]
        yield
        lane = lax.broadcasted_iota(jnp.int32, (tile, LANES), 1)
        low_head = lane < (LANES // HEADS_PER_VREG)
        for g in range(n_pairs):
            vg = vn[:, g * LANES:(g + 1) * LANES]
            v_lo = jnp.where(low_head, vg, 0.0).astype(bf16)
            v_hi = jnp.where(low_head, 0.0, vg).astype(bf16)
            rhs = jnp.concatenate(
                [jnp.concatenate([v_lo[n * GMLP_BLOCK:(n + 1) * GMLP_BLOCK, :]
                                  for n in range(n_blocks)], axis=1),
                 jnp.concatenate([v_hi[n * GMLP_BLOCK:(n + 1) * GMLP_BLOCK, :]
                                  for n in range(n_blocks)], axis=1)],
                axis=0)
            mixed = jnp.dot(ws_pair_ref[g], rhs, preferred_element_type=f32)
            for n in range(n_blocks):
                yb_ref[n * GMLP_BLOCK:(n + 1) * GMLP_BLOCK, g * LANES:(g + 1) * LANES] = (
                    mixed[:, n * LANES:(n + 1) * LANES])
        yield
        u = in_proj(3 * c, gw)
        bias = jnp.concatenate([bias_ref[...]] * n_blocks, axis=0)
        y_b = _rms(u * (yb_ref[...] + bias), gg_ref[...]).astype(bf16)
        yield

        y = jnp.concatenate([y_a, y_b], axis=-1)
        x1 = x + jnp.dot(y, w_out_ref[...], preferred_element_type=f32)
        x1_ref[...] = x1
        h2_ref[...] = (x1 * g2_ref[...]).astype(bf16)
        r2 = _inv_rms(x1)
        rr2_ref[...] = jnp.broadcast_to(r2 * r2, rr2_ref.shape)
        yield

    def stage_b():
        rr2 = jnp.concatenate([rr2_ref[...]] * (ff_chunk // LANES), axis=1)
        acc = None
        for k in range(N_FF_CHUNKS):
            cols = slice(k * ff_chunk, (k + 1) * ff_chunk)
            a = jnp.maximum(
                jnp.dot(h2_ref[...], w_up_ref[:, cols], preferred_element_type=f32), 0.0)
            a = (a * a * rr2).astype(bf16)
            yield
            if acc is None:
                acc = x1_ref[...]
            acc = acc + jnp.dot(a, w_down_ref[cols, :], preferred_element_type=f32)
            if k == N_FF_CHUNKS - 1:
                pre_ref[...] = acc
                sq = acc * acc
                ssq_ref[...] = sum(sq[:, j:j + LANES] for j in range(0, sq.shape[1], LANES))
            yield

    def stage_c():
        pre = pre_ref[...]
        ms = jnp.sum(ssq_ref[...], axis=-1, keepdims=True) * (1.0 / pre.shape[1])
        o_ref[...] = pre * lax.rsqrt(ms + EPS) * gf_ref[...]

    def run(order):
        gens = {}
        for name in order:
            if name == "c":
                stage_c()
                continue
            if name not in gens:
                gens[name] = {"a": stage_a, "b": stage_b}[name]()
            next(gens[name])
        for gen in gens.values():
            assert next(gen, "done") == "done"

    @pl.when(step == 0)
    def _():
        prepare()
        run("aaaaaaa")

    @pl.when((step > 0) & (step < n_tiles))
    def _():
        run("bc" "a" "aba" "ba" "bb" "aa" "bb" "ab")

    @pl.when(step == n_tiles)
    def _():
        run("bc" "bbbbbbb")

    @pl.when(step == n_tiles + 1)
    def _():
        run("c")


def _resident(a):
    return pl.BlockSpec(a.shape, lambda i: (0,) * a.ndim, pipeline_mode=pl.Buffered(1))


def kernel(x, norm1_g, w_in, conv_w, gmlp_ln_g, gmlp_ln_b, gmlp_ws, gmlp_bs, out_norm_conv_g,
           out_norm_gmlp_g, w_out, norm2_g, w_up, w_down, final_g):
    depth = w_in.shape[0]
    batch, seq, d_model = x.shape
    c = conv_w.shape[1]
    gw = gmlp_ln_g.shape[1]
    head_dim = gw // GMLP_HEADS
    assert depth == 1, "one fused block per call"
    assert seq % TILE == 0 and TILE % GMLP_BLOCK == 0
    assert head_dim * HEADS_PER_VREG == LANES
    assert w_up.shape[2] % N_FF_CHUNKS == 0
    tiles_per_batch = seq // TILE
    n_tiles = batch * tiles_per_batch
    bf16 = jnp.bfloat16
    row = lambda a: a.reshape(1, -1)
    bias = jnp.repeat(jnp.transpose(gmlp_bs[0]), head_dim, axis=1)

    x_spec = pl.BlockSpec((TILE, d_model), lambda i: (jnp.minimum(i, n_tiles - 1), 0))
    o_spec = pl.BlockSpec((TILE, d_model), lambda i: (jnp.maximum(i - 2, 0), 0))
    in_hbm = pl.BlockSpec(memory_space=pl.ANY)
    operands_and_specs = (
        (x.reshape(batch * seq, d_model), x_spec),
        (row(norm1_g[0]), None), (w_in[0], in_hbm), (jnp.transpose(conv_w[0]), None),
        (row(gmlp_ln_g[0]), None), (row(gmlp_ln_b[0]), None), (gmlp_ws[0], None), (bias, None),
        (row(out_norm_conv_g[0]), None), (row(out_norm_gmlp_g[0]), None), (w_out[0], in_hbm),
        (row(norm2_g[0]), None), (w_up[0], in_hbm), (w_down[0], in_hbm), (row(final_g), None),
    )
    operands = [a for a, _ in operands_and_specs]
    in_specs = [_resident(a) if spec is None else spec for a, spec in operands_and_specs]
    out = pl.pallas_call(
        functools.partial(_block_kernel, n_tiles=n_tiles, tiles_per_batch=tiles_per_batch),
        grid=(n_tiles + 2,),
        in_specs=in_specs,
        out_specs=o_spec,
        out_shape=jax.ShapeDtypeStruct((batch * seq, d_model), x.dtype),
        scratch_shapes=[
            pltpu.VMEM(w_in.shape[1:], bf16),
            pltpu.VMEM(w_out.shape[1:], bf16),
            pltpu.VMEM(w_up.shape[1:], bf16),
            pltpu.VMEM(w_down.shape[1:], bf16),
            pltpu.VMEM((GMLP_HEADS // HEADS_PER_VREG, GMLP_BLOCK, HEADS_PER_VREG * GMLP_BLOCK), bf16),
            pltpu.VMEM((TILE + SUBLANES, c), jnp.float32),
            pltpu.VMEM((TILE, gw), jnp.float32),
            pltpu.VMEM((TILE, d_model), jnp.float32),
            pltpu.VMEM((TILE, d_model), bf16),
            pltpu.VMEM((TILE, LANES), jnp.float32),
            pltpu.VMEM((TILE, d_model), jnp.float32),
            pltpu.VMEM((TILE, LANES), jnp.float32),
        ],
        compiler_params=pltpu.CompilerParams(
            dimension_semantics=("arbitrary",),
            vmem_limit_bytes=VMEM_LIMIT_BYTES),
        name="hybrid_block",
    )(*operands)
    return out.reshape(batch, seq, d_model)
```

```python
import functools

import jax
import jax.numpy as jnp
from jax import lax
from jax.experimental import pallas as pl
from jax.experimental.pallas import tpu as pltpu

EPS = 1e-6
CHUNK = 64
GMLP_BLOCK = 128
GMLP_HEADS = 8
LANES = 128
SUBLANES = 8
HEADS_PER_VREG = 2

TILE = 512
N_FF_CHUNKS = 4
VMEM_LIMIT_BYTES = 56 * 1024 * 1024

WEIGHT_CHUNK_BYTES = 1024 * 1024
WEIGHT_LOAD_SLOTS = 8


def _inv_rms(x):
    return lax.rsqrt(jnp.mean(x * x, axis=-1, keepdims=True) + EPS)


def _rms(x, g):
    return x * _inv_rms(x) * g


def _load_weight_as_bf16(src_hbm, dst_ref):
    n_rows, n_cols = src_hbm.shape
    rows = 1 << ((WEIGHT_CHUNK_BYTES // (n_cols * 4)).bit_length() - 1)
    n_chunks = n_rows // rows
    assert rows * n_chunks == n_rows

    def body(stage_ref, sem_ref):
        def chunk_copy(i, slot):
            return pltpu.make_async_copy(
                src_hbm.at[pl.ds(i * rows, rows), :], stage_ref.at[slot], sem_ref.at[slot])

        for i in range(min(WEIGHT_LOAD_SLOTS - 1, n_chunks)):
            chunk_copy(i, i).start()

        def step(i, carry):
            slot = i % WEIGHT_LOAD_SLOTS
            ahead = i + WEIGHT_LOAD_SLOTS - 1

            @pl.when(ahead < n_chunks)
            def _():
                chunk_copy(ahead, ahead % WEIGHT_LOAD_SLOTS).start()

            chunk_copy(i, slot).wait()
            dst_ref[pl.ds(pl.multiple_of(i * rows, rows), rows), :] = (
                stage_ref[slot].astype(jnp.bfloat16))
            return carry

        lax.fori_loop(0, n_chunks, step, 0)

    pl.run_scoped(body,
                  pltpu.VMEM((WEIGHT_LOAD_SLOTS, rows, n_cols), jnp.float32),
                  pltpu.SemaphoreType.DMA((WEIGHT_LOAD_SLOTS,)))


def _block_kernel(x_ref, g1_ref, w_in_hbm, conv_w_ref, ln_g_ref, ln_b_ref, ws_ref, bias_ref,
                  gc_ref, gg_ref, w_out_hbm, g2_ref, w_up_hbm, w_down_hbm, gf_ref,
                  o_ref, w_in_ref, w_out_ref, w_up_ref, w_down_ref,
                  ws_pair_ref, zbuf_ref, yb_ref, x1_ref, h2_ref, rr2_ref, pre_ref, ssq_ref,
                  *, n_tiles, tiles_per_batch):
    f32, bf16 = jnp.float32, jnp.bfloat16
    tile = x_ref.shape[0]
    c = conv_w_ref.shape[1]
    gw = ln_g_ref.shape[1]
    ff_chunk = w_up_ref.shape[1] // N_FF_CHUNKS
    n_blocks = tile // GMLP_BLOCK
    n_pairs = GMLP_HEADS // HEADS_PER_VREG
    step = pl.program_id(0)

    def prepare():
        _load_weight_as_bf16(w_in_hbm, w_in_ref)
        _load_weight_as_bf16(w_out_hbm, w_out_ref)
        _load_weight_as_bf16(w_up_hbm, w_up_ref)
        _load_weight_as_bf16(w_down_hbm, w_down_ref)
        qi = lax.broadcasted_iota(jnp.int32, (GMLP_BLOCK, GMLP_BLOCK), 0) // CHUNK
        kj = lax.broadcasted_iota(jnp.int32, (GMLP_BLOCK, GMLP_BLOCK), 1) // CHUNK
        causal = kj <= qi
        for h in range(GMLP_HEADS):
            g, r = divmod(h, HEADS_PER_VREG)
            ws_pair_ref[g, :, r * GMLP_BLOCK:(r + 1) * GMLP_BLOCK] = (
                jnp.where(causal, ws_ref[h], 0.0).astype(bf16))
        zbuf_ref[0:SUBLANES, :] = jnp.zeros((SUBLANES, c), f32)
        pre_ref[...] = jnp.zeros(pre_ref.shape, f32)
        ssq_ref[...] = jnp.zeros(ssq_ref.shape, f32)

    def stage_a():
        x = x_ref[...]
        h = _rms(x, g1_ref[...]).astype(bf16)
        yield

        def in_proj(lo, width):
            return jnp.dot(h, w_in_ref[:, lo:lo + width], preferred_element_type=f32)

        z = in_proj(c, c) * in_proj(2 * c, c)
        zbuf_ref[SUBLANES:SUBLANES + tile, :] = z
        cw = conv_w_ref[...]
        conv = (zbuf_ref[SUBLANES - 2:SUBLANES - 2 + tile, :] * cw[0:1, :]
                + zbuf_ref[SUBLANES - 1:SUBLANES - 1 + tile, :] * cw[1:2, :]
                + z * cw[2:3, :])
        next_starts_sequence = (step + 1) % tiles_per_batch == 0
        zbuf_ref[0:SUBLANES, :] = jnp.where(next_starts_sequence, 0.0,
                                            zbuf_ref[tile:tile + SUBLANES, :])
        yield
        y_a = _rms(in_proj(0, c) * conv, gc_ref[...]).astype(bf16)
        yield

        v = in_proj(3 * c + gw, gw)
        mu = jnp.mean(v, axis=-1, keepdims=True)
        vc = v - mu
        vn = vc * lax.rsqrt(jnp.mean(vc * vc, axis=-1, keepdims=True) + EPS)
        vn = vn * ln_g_ref[...] + ln_b_ref[...]
        yield
        lane = lax.broadcasted_iota(jnp.int32, (tile, LANES), 1)
        low_head = lane < (LANES // HEADS_PER_VREG)
        for g in range(n_pairs):
            vg = vn[:, g * LANES:(g + 1) * LANES]
            v_lo = jnp.where(low_head, vg, 0.0).astype(bf16)
            v_hi = jnp.where(low_head, 0.0, vg).astype(bf16)
            rhs = jnp.concatenate(
                [jnp.concatenate([v_lo[n * GMLP_BLOCK:(n + 1) * GMLP_BLOCK, :]
                                  for n in range(n_blocks)], axis=1),
                 jnp.concatenate([v_hi[n * GMLP_BLOCK:(n + 1) * GMLP_BLOCK, :]
                                  for n in range(n_blocks)], axis=1)],
                axis=0)
            mixed = jnp.dot(ws_pair_ref[g], rhs, preferred_element_type=f32)
            for n in range(n_blocks):
                yb_ref[n * GMLP_BLOCK:(n + 1) * GMLP_BLOCK, g * LANES:(g + 1) * LANES] = (
                    mixed[:, n * LANES:(n + 1) * LANES])
        yield
        u = in_proj(3 * c, gw)
        bias = jnp.concatenate([bias_ref[...]] * n_blocks, axis=0)
        y_b = _rms(u * (yb_ref[...] + bias), gg_ref[...]).astype(bf16)
        yield

        y = jnp.concatenate([y_a, y_b], axis=-1)
        x1 = x + jnp.dot(y, w_out_ref[...], preferred_element_type=f32)
        x1_ref[...] = x1
        h2_ref[...] = (x1 * g2_ref[...]).astype(bf16)
        r2 = _inv_rms(x1)
        rr2_ref[...] = jnp.broadcast_to(r2 * r2, rr2_ref.shape)
        yield

    def stage_b():
        rr2 = jnp.concatenate([rr2_ref[...]] * (ff_chunk // LANES), axis=1)
        acc = None
        for k in range(N_FF_CHUNKS):
            cols = slice(k * ff_chunk, (k + 1) * ff_chunk)
            a = jnp.maximum(
                jnp.dot(h2_ref[...], w_up_ref[:, cols], preferred_element_type=f32), 0.0)
            a = (a * a * rr2).astype(bf16)
            yield
            if acc is None:
                acc = x1_ref[...]
            acc = acc + jnp.dot(a, w_down_ref[cols, :], preferred_element_type=f32)
            if k == N_FF_CHUNKS - 1:
                pre_ref[...] = acc
                sq = acc * acc
                ssq_ref[...] = sum(sq[:, j:j + LANES] for j in range(0, sq.shape[1], LANES))
            yield

    def stage_c():
        pre = pre_ref[...]
        ms = jnp.sum(ssq_ref[...], axis=-1, keepdims=True) * (1.0 / pre.shape[1])
        o_ref[...] = pre * lax.rsqrt(ms + EPS) * gf_ref[...]

    def run(order):
        gens = {}
        for name in order:
            if name == "c":
                stage_c()
                continue
            if name not in gens:
                gens[name] = {"a": stage_a, "b": stage_b}[name]()
            next(gens[name])
        for gen in gens.values():
            assert next(gen, "done") == "done"

    @pl.when(step == 0)
    def _():
        prepare()
        run("aaaaaaa")

    @pl.when((step > 0) & (step < n_tiles))
    def _():
        run("bc" "a" "aba" "ba" "bb" "aa" "bb" "ab")

    @pl.when(step == n_tiles)
    def _():
        run("bc" "bbbbbbb")

    @pl.when(step == n_tiles + 1)
    def _():
        run("c")


def _resident(a):
    return pl.BlockSpec(a.shape, lambda i: (0,) * a.ndim, pipeline_mode=pl.Buffered(1))


def kernel(x, norm1_g, w_in, conv_w, gmlp_ln_g, gmlp_ln_b, gmlp_ws, gmlp_bs, out_norm_conv_g,
           out_norm_gmlp_g, w_out, norm2_g, w_up, w_down, final_g):
    depth = w_in.shape[0]
    batch, seq, d_model = x.shape
    c = conv_w.shape[1]
    gw = gmlp_ln_g.shape[1]
    head_dim = gw // GMLP_HEADS
    assert depth == 1, "one fused block per call"
    assert seq % TILE == 0 and TILE % GMLP_BLOCK == 0
    assert head_dim * HEADS_PER_VREG == LANES
    assert w_up.shape[2] % N_FF_CHUNKS == 0
    tiles_per_batch = seq // TILE
    n_tiles = batch * tiles_per_batch
    bf16 = jnp.bfloat16
    row = lambda a: a.reshape(1, -1)
    bias = jnp.repeat(jnp.transpose(gmlp_bs[0]), head_dim, axis=1)

    x_spec = pl.BlockSpec((TILE, d_model), lambda i: (jnp.minimum(i, n_tiles - 1), 0))
    o_spec = pl.BlockSpec((TILE, d_model), lambda i: (jnp.maximum(i - 2, 0), 0))
    in_hbm = pl.BlockSpec(memory_space=pl.ANY)
    operands_and_specs = (
        (x.reshape(batch * seq, d_model), x_spec),
        (row(norm1_g[0]), None), (w_in[0], in_hbm), (jnp.transpose(conv_w[0]), None),
        (row(gmlp_ln_g[0]), None), (row(gmlp_ln_b[0]), None), (gmlp_ws[0], None), (bias, None),
        (row(out_norm_conv_g[0]), None), (row(out_norm_gmlp_g[0]), None), (w_out[0], in_hbm),
        (row(norm2_g[0]), None), (w_up[0], in_hbm), (w_down[0], in_hbm), (row(final_g), None),
    )
    operands = [a for a, _ in operands_and_specs]
    in_specs = [_resident(a) if spec is None else spec for a, spec in operands_and_specs]
    out = pl.pallas_call(
        functools.partial(_block_kernel, n_tiles=n_tiles, tiles_per_batch=tiles_per_batch),
        grid=(n_tiles + 2,),
        in_specs=in_specs,
        out_specs=o_spec,
        out_shape=jax.ShapeDtypeStruct((batch * seq, d_model), x.dtype),
        scratch_shapes=[
            pltpu.VMEM(w_in.shape[1:], bf16),
            pltpu.VMEM(w_out.shape[1:], bf16),
            pltpu.VMEM(w_up.shape[1:], bf16),
            pltpu.VMEM(w_down.shape[1:], bf16),
            pltpu.VMEM((GMLP_HEADS // HEADS_PER_VREG, GMLP_BLOCK, HEADS_PER_VREG * GMLP_BLOCK), bf16),
            pltpu.VMEM((TILE + SUBLANES, c), jnp.float32),
            pltpu.VMEM((TILE, gw), jnp.float32),
            pltpu.VMEM((TILE, d_model), jnp.float32),
            pltpu.VMEM((TILE, d_model), bf16),
            pltpu.VMEM((TILE, LANES), jnp.float32),
            pltpu.VMEM((TILE, d_model), jnp.float32),
            pltpu.VMEM((TILE, LANES), jnp.float32),
        ],
        compiler_params=pltpu.CompilerParams(
            dimension_semantics=("arbitrary",),
            vmem_limit_bytes=VMEM_LIMIT_BYTES),
        name="hybrid_block",
    )(*operands)
    return out.reshape(batch, seq, d_model)
```

```python
import functools

import jax
import jax.numpy as jnp
from jax import lax
from jax.experimental import pallas as pl
from jax.experimental.pallas import tpu as pltpu

EPS = 1e-6
CHUNK = 64
GMLP_BLOCK = 128
GMLP_HEADS = 8
LANES = 128
SUBLANES = 8
HEADS_PER_VREG = 2

TILE = 512
N_FF_CHUNKS = 4
VMEM_LIMIT_BYTES = 56 * 1024 * 1024

WEIGHT_CHUNK_BYTES = 1024 * 1024
WEIGHT_LOAD_SLOTS = 8
OVERLAPPED_LOAD_SLOTS = 4
OVERLAPPED_CHUNKS_PER_GROUP = 2


def _inv_rms(x):
    return lax.rsqrt(jnp.mean(x * x, axis=-1, keepdims=True) + EPS)


def _rms(x, g):
    return x * _inv_rms(x) * g


class _WeightStream:
    def __init__(self, src_hbm, dst_ref, stage_ref, sem_ref):
        self.src_hbm, self.dst_ref, self.stage_ref, self.sem_ref = src_hbm, dst_ref, stage_ref, sem_ref
        self.slots, self.rows, _ = stage_ref.shape
        assert src_hbm.shape[0] % self.rows == 0
        self.n_chunks = src_hbm.shape[0] // self.rows
        self.done = 0

    @staticmethod
    def staging(src_hbm, slots):
        n_cols = src_hbm.shape[1]
        rows = 1 << ((WEIGHT_CHUNK_BYTES // (n_cols * 4)).bit_length() - 1)
        return (pltpu.VMEM((slots, rows, n_cols), jnp.float32), pltpu.SemaphoreType.DMA((slots,)))

    def _copy(self, i, slot):
        return pltpu.make_async_copy(self.src_hbm.at[pl.ds(i * self.rows, self.rows), :],
                                     self.stage_ref.at[slot], self.sem_ref.at[slot])

    def start(self):
        for i in range(min(self.slots - 1, self.n_chunks)):
            self._copy(i, i).start()

    def advance(self, count):
        first, last = self.done, min(self.done + count, self.n_chunks)

        def step(i, carry):
            ahead = i + self.slots - 1

            @pl.when(ahead < self.n_chunks)
            def _():
                self._copy(ahead, ahead % self.slots).start()

            slot = i % self.slots
            self._copy(i, slot).wait()
            self.dst_ref[pl.ds(pl.multiple_of(i * self.rows, self.rows), self.rows), :] = (
                self.stage_ref[slot].astype(jnp.bfloat16))
            return carry

        lax.fori_loop(first, last, step, 0)
        self.done = last

    def finish(self):
        self.advance(self.n_chunks - self.done)


def _load_weight_as_bf16(src_hbm, dst_ref):
    def body(stage_ref, sem_ref):
        stream = _WeightStream(src_hbm, dst_ref, stage_ref, sem_ref)
        stream.start()
        stream.finish()

    pl.run_scoped(body, *_WeightStream.staging(src_hbm, WEIGHT_LOAD_SLOTS))


def _block_kernel(x_ref, g1_ref, w_in_hbm, conv_w_ref, ln_g_ref, ln_b_ref, ws_ref, bias_ref,
                  gc_ref, gg_ref, w_out_hbm, g2_ref, w_up_hbm, w_down_hbm, gf_ref,
                  o_ref, w_in_ref, w_out_ref, w_up_ref, w_down_ref,
                  ws_pair_ref, zbuf_ref, yb_ref, x1_ref, h2_ref, rr2_ref, pre_ref, ssq_ref,
                  *, n_tiles, tiles_per_batch):
    f32, bf16 = jnp.float32, jnp.bfloat16
    tile = x_ref.shape[0]
    c = conv_w_ref.shape[1]
    gw = ln_g_ref.shape[1]
    ff_chunk = w_up_ref.shape[1] // N_FF_CHUNKS
    n_blocks = tile // GMLP_BLOCK
    n_pairs = GMLP_HEADS // HEADS_PER_VREG
    step = pl.program_id(0)

    def prepare():
        qi = lax.broadcasted_iota(jnp.int32, (GMLP_BLOCK, GMLP_BLOCK), 0) // CHUNK
        kj = lax.broadcasted_iota(jnp.int32, (GMLP_BLOCK, GMLP_BLOCK), 1) // CHUNK
        causal = kj <= qi
        for h in range(GMLP_HEADS):
            g, r = divmod(h, HEADS_PER_VREG)
            ws_pair_ref[g, :, r * GMLP_BLOCK:(r + 1) * GMLP_BLOCK] = (
                jnp.where(causal, ws_ref[h], 0.0).astype(bf16))
        zbuf_ref[0:SUBLANES, :] = jnp.zeros((SUBLANES, c), f32)
        pre_ref[...] = jnp.zeros(pre_ref.shape, f32)
        ssq_ref[...] = jnp.zeros(ssq_ref.shape, f32)

    def stage_a():
        x = x_ref[...]
        h = _rms(x, g1_ref[...]).astype(bf16)
        yield

        def in_proj(lo, width):
            return jnp.dot(h, w_in_ref[:, lo:lo + width], preferred_element_type=f32)

        z = in_proj(c, c) * in_proj(2 * c, c)
        zbuf_ref[SUBLANES:SUBLANES + tile, :] = z
        cw = conv_w_ref[...]
        conv = (zbuf_ref[SUBLANES - 2:SUBLANES - 2 + tile, :] * cw[0:1, :]
                + zbuf_ref[SUBLANES - 1:SUBLANES - 1 + tile, :] * cw[1:2, :]
                + z * cw[2:3, :])
        next_starts_sequence = (step + 1) % tiles_per_batch == 0
        zbuf_ref[0:SUBLANES, :] = jnp.where(next_starts_sequence, 0.0,
                                            zbuf_ref[tile:tile + SUBLANES, :])
        yield
        y_a = _rms(in_proj(0, c) * conv, gc_ref[...]).astype(bf16)
        yield

        v = in_proj(3 * c + gw, gw)
        mu = jnp.mean(v, axis=-1, keepdims=True)
        vc = v - mu
        vn = vc * lax.rsqrt(jnp.mean(vc * vc, axis=-1, keepdims=True) + EPS)
        vn = vn * ln_g_ref[...] + ln_b_ref[...]
        yield
        lane = lax.broadcasted_iota(jnp.int32, (tile, LANES), 1)
        low_head = lane < (LANES // HEADS_PER_VREG)
        for g in range(n_pairs):
            vg = vn[:, g * LANES:(g + 1) * LANES]
            v_lo = jnp.where(low_head, vg, 0.0).astype(bf16)
            v_hi = jnp.where(low_head, 0.0, vg).astype(bf16)
            rhs = jnp.concatenate(
                [jnp.concatenate([v_lo[n * GMLP_BLOCK:(n + 1) * GMLP_BLOCK, :]
                                  for n in range(n_blocks)], axis=1),
                 jnp.concatenate([v_hi[n * GMLP_BLOCK:(n + 1) * GMLP_BLOCK, :]
                                  for n in range(n_blocks)], axis=1)],
                axis=0)
            mixed = jnp.dot(ws_pair_ref[g], rhs, preferred_element_type=f32)
            for n in range(n_blocks):
                yb_ref[n * GMLP_BLOCK:(n + 1) * GMLP_BLOCK, g * LANES:(g + 1) * LANES] = (
                    mixed[:, n * LANES:(n + 1) * LANES])
        yield
        u = in_proj(3 * c, gw)
        bias = jnp.concatenate([bias_ref[...]] * n_blocks, axis=0)
        y_b = _rms(u * (yb_ref[...] + bias), gg_ref[...]).astype(bf16)
        yield

        y = jnp.concatenate([y_a, y_b], axis=-1)
        x1 = x + jnp.dot(y, w_out_ref[...], preferred_element_type=f32)
        x1_ref[...] = x1
        h2_ref[...] = (x1 * g2_ref[...]).astype(bf16)
        r2 = _inv_rms(x1)
        rr2_ref[...] = jnp.broadcast_to(r2 * r2, rr2_ref.shape)
        yield

    def stage_b():
        rr2 = jnp.concatenate([rr2_ref[...]] * (ff_chunk // LANES), axis=1)
        acc = None
        for k in range(N_FF_CHUNKS):
            cols = slice(k * ff_chunk, (k + 1) * ff_chunk)
            a = jnp.maximum(
                jnp.dot(h2_ref[...], w_up_ref[:, cols], preferred_element_type=f32), 0.0)
            a = (a * a * rr2).astype(bf16)
            yield
            if acc is None:
                acc = x1_ref[...]
            acc = acc + jnp.dot(a, w_down_ref[cols, :], preferred_element_type=f32)
            if k == N_FF_CHUNKS - 1:
                pre_ref[...] = acc
                sq = acc * acc
                ssq_ref[...] = sum(sq[:, j:j + LANES] for j in range(0, sq.shape[1], LANES))
            yield

    def stage_c():
        pre = pre_ref[...]
        ms = jnp.sum(ssq_ref[...], axis=-1, keepdims=True) * (1.0 / pre.shape[1])
        o_ref[...] = pre * lax.rsqrt(ms + EPS) * gf_ref[...]

    def run(order, after_group=None):
        gens = {}
        for name in order:
            if name == "c":
                stage_c()
                continue
            if name not in gens:
                gens[name] = {"a": stage_a, "b": stage_b}[name]()
            next(gens[name])
            if after_group is not None:
                after_group()
        for gen in gens.values():
            assert next(gen, "done") == "done"

    @pl.when(step == 0)
    def _():
        _load_weight_as_bf16(w_in_hbm, w_in_ref)
        _load_weight_as_bf16(w_out_hbm, w_out_ref)
        prepare()

        def mix_while_loading(up_stage, up_sem, down_stage, down_sem):
            streams = (_WeightStream(w_up_hbm, w_up_ref, up_stage, up_sem),
                       _WeightStream(w_down_hbm, w_down_ref, down_stage, down_sem))
            for stream in streams:
                stream.start()

            def drain_some():
                for stream in streams:
                    stream.advance(OVERLAPPED_CHUNKS_PER_GROUP)

            run("aaaaaaa", after_group=drain_some)
            for stream in streams:
                stream.finish()

        pl.run_scoped(mix_while_loading,
                      *_WeightStream.staging(w_up_hbm, OVERLAPPED_LOAD_SLOTS),
                      *_WeightStream.staging(w_down_hbm, OVERLAPPED_LOAD_SLOTS))

    @pl.when((step > 0) & (step < n_tiles))
    def _():
        run("bc" "a" "aba" "ba" "bb" "aa" "bb" "ab")

    @pl.when(step == n_tiles)
    def _():
        run("bc" "bbbbbbb")

    @pl.when(step == n_tiles + 1)
    def _():
        run("c")


def _resident(a):
    return pl.BlockSpec(a.shape, lambda i: (0,) * a.ndim, pipeline_mode=pl.Buffered(1))


def kernel(x, norm1_g, w_in, conv_w, gmlp_ln_g, gmlp_ln_b, gmlp_ws, gmlp_bs, out_norm_conv_g,
           out_norm_gmlp_g, w_out, norm2_g, w_up, w_down, final_g):
    depth = w_in.shape[0]
    batch, seq, d_model = x.shape
    c = conv_w.shape[1]
    gw = gmlp_ln_g.shape[1]
    head_dim = gw // GMLP_HEADS
    assert depth == 1, "one fused block per call"
    assert seq % TILE == 0 and TILE % GMLP_BLOCK == 0
    assert head_dim * HEADS_PER_VREG == LANES
    assert w_up.shape[2] % N_FF_CHUNKS == 0
    tiles_per_batch = seq // TILE
    n_tiles = batch * tiles_per_batch
    bf16 = jnp.bfloat16
    row = lambda a: a.reshape(1, -1)
    bias = jnp.repeat(jnp.transpose(gmlp_bs[0]), head_dim, axis=1)

    x_spec = pl.BlockSpec((TILE, d_model), lambda i: (jnp.minimum(i, n_tiles - 1), 0))
    o_spec = pl.BlockSpec((TILE, d_model), lambda i: (jnp.maximum(i - 2, 0), 0))
    in_hbm = pl.BlockSpec(memory_space=pl.ANY)
    operands_and_specs = (
        (x.reshape(batch * seq, d_model), x_spec),
        (row(norm1_g[0]), None), (w_in[0], in_hbm), (jnp.transpose(conv_w[0]), None),
        (row(gmlp_ln_g[0]), None), (row(gmlp_ln_b[0]), None), (gmlp_ws[0], None), (bias, None),
        (row(out_norm_conv_g[0]), None), (row(out_norm_gmlp_g[0]), None), (w_out[0], in_hbm),
        (row(norm2_g[0]), None), (w_up[0], in_hbm), (w_down[0], in_hbm), (row(final_g), None),
    )
    operands = [a for a, _ in operands_and_specs]
    in_specs = [_resident(a) if spec is None else spec for a, spec in operands_and_specs]
    out = pl.pallas_call(
        functools.partial(_block_kernel, n_tiles=n_tiles, tiles_per_batch=tiles_per_batch),
        grid=(n_tiles + 2,),
        in_specs=in_specs,
        out_specs=o_spec,
        out_shape=jax.ShapeDtypeStruct((batch * seq, d_model), x.dtype),
        scratch_shapes=[
            pltpu.VMEM(w_in.shape[1:], bf16),
            pltpu.VMEM(w_out.shape[1:], bf16),
            pltpu.VMEM(w_up.shape[1:], bf16),
            pltpu.VMEM(w_down.shape[1:], bf16),
            pltpu.VMEM((GMLP_HEADS // HEADS_PER_VREG, GMLP_BLOCK, HEADS_PER_VREG * GMLP_BLOCK), bf16),
            pltpu.VMEM((TILE + SUBLANES, c), jnp.float32),
            pltpu.VMEM((TILE, gw), jnp.float32),
            pltpu.VMEM((TILE, d_model), jnp.float32),
            pltpu.VMEM((TILE, d_model), bf16),
            pltpu.VMEM((TILE, LANES), jnp.float32),
            pltpu.VMEM((TILE, d_model), jnp.float32),
            pltpu.VMEM((TILE, LANES), jnp.float32),
        ],
        compiler_params=pltpu.CompilerParams(
            dimension_semantics=("arbitrary",),
            vmem_limit_bytes=VMEM_LIMIT_BYTES),
        name="hybrid_block",
    )(*operands)
    return out.reshape(batch, seq, d_model)
```

```python
import functools

import jax
import jax.numpy as jnp
from jax import lax
from jax.experimental import pallas as pl
from jax.experimental.pallas import tpu as pltpu

EPS = 1e-6
CHUNK = 64
GMLP_BLOCK = 128
GMLP_HEADS = 8
LANES = 128
SUBLANES = 8
HEADS_PER_VREG = 2

TILE = 512
TILES_PER_STEP = 2
N_FF_CHUNKS = 4
VMEM_LIMIT_BYTES = 60000 * 1024

WEIGHT_CHUNK_BYTES = 1024 * 1024
WEIGHT_LOAD_SLOTS = 6
OVERLAPPED_LOAD_SLOTS = 3
OVERLAPPED_CHUNKS_PER_GROUP = 2


def _inv_rms(x):
    return lax.rsqrt(jnp.mean(x * x, axis=-1, keepdims=True) + EPS)


def _rms(x, g):
    return x * _inv_rms(x) * g


class _WeightStream:
    def __init__(self, src_hbm, dst_ref, stage_ref, sem_ref):
        self.src_hbm, self.dst_ref, self.stage_ref, self.sem_ref = src_hbm, dst_ref, stage_ref, sem_ref
        self.slots, self.rows, _ = stage_ref.shape
        assert src_hbm.shape[0] % self.rows == 0
        self.n_chunks = src_hbm.shape[0] // self.rows
        self.done = 0

    @staticmethod
    def staging(src_hbm, slots):
        n_cols = src_hbm.shape[1]
        rows = 1 << ((WEIGHT_CHUNK_BYTES // (n_cols * 4)).bit_length() - 1)
        return (pltpu.VMEM((slots, rows, n_cols), jnp.float32), pltpu.SemaphoreType.DMA((slots,)))

    def _copy(self, i, slot):
        return pltpu.make_async_copy(self.src_hbm.at[pl.ds(i * self.rows, self.rows), :],
                                     self.stage_ref.at[slot], self.sem_ref.at[slot])

    def start(self):
        for i in range(min(self.slots - 1, self.n_chunks)):
            self._copy(i, i).start()

    def advance(self, count):
        first, last = self.done, min(self.done + count, self.n_chunks)

        def step(i, carry):
            ahead = i + self.slots - 1

            @pl.when(ahead < self.n_chunks)
            def _():
                self._copy(ahead, ahead % self.slots).start()

            slot = i % self.slots
            self._copy(i, slot).wait()
            self.dst_ref[pl.ds(pl.multiple_of(i * self.rows, self.rows), self.rows), :] = (
                self.stage_ref[slot].astype(jnp.bfloat16))
            return carry

        lax.fori_loop(first, last, step, 0)
        self.done = last

    def finish(self):
        self.advance(self.n_chunks - self.done)


def _load_weight_as_bf16(src_hbm, dst_ref):
    def body(stage_ref, sem_ref):
        stream = _WeightStream(src_hbm, dst_ref, stage_ref, sem_ref)
        stream.start()
        stream.finish()

    pl.run_scoped(body, *_WeightStream.staging(src_hbm, WEIGHT_LOAD_SLOTS))


def _block_kernel(x_ref, g1_ref, w_in_hbm, conv_w_ref, ln_g_ref, ln_b_ref, ws_ref, bias_ref,
                  gc_ref, gg_ref, w_out_hbm, g2_ref, w_up_hbm, w_down_hbm, gf_ref,
                  o_ref, w_in_ref, w_out_ref, w_up_ref, w_down_ref,
                  ws_pair_ref, zbuf_ref, yb_ref, x1_ref, h2_ref, rr2_ref, pre_ref, ssq_ref,
                  *, n_tiles, tiles_per_batch):
    f32, bf16 = jnp.float32, jnp.bfloat16
    tile = TILE
    c = conv_w_ref.shape[1]
    gw = ln_g_ref.shape[1]
    ff_chunk = w_up_ref.shape[1] // N_FF_CHUNKS
    n_blocks = tile // GMLP_BLOCK
    n_pairs = GMLP_HEADS // HEADS_PER_VREG
    step = rows = None

    def prepare():
        qi = lax.broadcasted_iota(jnp.int32, (GMLP_BLOCK, GMLP_BLOCK), 0) // CHUNK
        kj = lax.broadcasted_iota(jnp.int32, (GMLP_BLOCK, GMLP_BLOCK), 1) // CHUNK
        causal = kj <= qi
        for h in range(GMLP_HEADS):
            g, r = divmod(h, HEADS_PER_VREG)
            ws_pair_ref[g, :, r * GMLP_BLOCK:(r + 1) * GMLP_BLOCK] = (
                jnp.where(causal, ws_ref[h], 0.0).astype(bf16))
        zbuf_ref[0:SUBLANES, :] = jnp.zeros((SUBLANES, c), f32)
        pre_ref[...] = jnp.zeros(pre_ref.shape, f32)
        ssq_ref[...] = jnp.zeros(ssq_ref.shape, f32)

    def stage_a():
        x = x_ref[rows, :]
        h = _rms(x, g1_ref[...]).astype(bf16)
        yield

        def in_proj(lo, width):
            return jnp.dot(h, w_in_ref[:, lo:lo + width], preferred_element_type=f32)

        z = in_proj(c, c) * in_proj(2 * c, c)
        zbuf_ref[SUBLANES:SUBLANES + tile, :] = z
        cw = conv_w_ref[...]
        conv = (zbuf_ref[SUBLANES - 2:SUBLANES - 2 + tile, :] * cw[0:1, :]
                + zbuf_ref[SUBLANES - 1:SUBLANES - 1 + tile, :] * cw[1:2, :]
                + z * cw[2:3, :])
        next_starts_sequence = (step + 1) % tiles_per_batch == 0
        zbuf_ref[0:SUBLANES, :] = jnp.where(next_starts_sequence, 0.0,
                                            zbuf_ref[tile:tile + SUBLANES, :])
        yield
        y_a = _rms(in_proj(0, c) * conv, gc_ref[...]).astype(bf16)
        yield

        v = in_proj(3 * c + gw, gw)
        mu = jnp.mean(v, axis=-1, keepdims=True)
        vc = v - mu
        vn = vc * lax.rsqrt(jnp.mean(vc * vc, axis=-1, keepdims=True) + EPS)
        vn = vn * ln_g_ref[...] + ln_b_ref[...]
        yield
        lane = lax.broadcasted_iota(jnp.int32, (tile, LANES), 1)
        low_head = lane < (LANES // HEADS_PER_VREG)
        for g in range(n_pairs):
            vg = vn[:, g * LANES:(g + 1) * LANES]
            v_lo = jnp.where(low_head, vg, 0.0).astype(bf16)
            v_hi = jnp.where(low_head, 0.0, vg).astype(bf16)
            rhs = jnp.concatenate(
                [jnp.concatenate([v_lo[n * GMLP_BLOCK:(n + 1) * GMLP_BLOCK, :]
                                  for n in range(n_blocks)], axis=1),
                 jnp.concatenate([v_hi[n * GMLP_BLOCK:(n + 1) * GMLP_BLOCK, :]
                                  for n in range(n_blocks)], axis=1)],
                axis=0)
            mixed = jnp.dot(ws_pair_ref[g], rhs, preferred_element_type=f32)
            for n in range(n_blocks):
                yb_ref[n * GMLP_BLOCK:(n + 1) * GMLP_BLOCK, g * LANES:(g + 1) * LANES] = (
                    mixed[:, n * LANES:(n + 1) * LANES])
        yield
        u = in_proj(3 * c, gw)
        bias = jnp.concatenate([bias_ref[...]] * n_blocks, axis=0)
        y_b = _rms(u * (yb_ref[...] + bias), gg_ref[...]).astype(bf16)
        yield

        y = jnp.concatenate([y_a, y_b], axis=-1)
        x1 = x + jnp.dot(y, w_out_ref[...], preferred_element_type=f32)
        x1_ref[...] = x1
        h2_ref[...] = (x1 * g2_ref[...]).astype(bf16)
        r2 = _inv_rms(x1)
        rr2_ref[...] = jnp.broadcast_to(r2 * r2, rr2_ref.shape)
        yield

    def stage_b():
        rr2 = jnp.concatenate([rr2_ref[...]] * (ff_chunk // LANES), axis=1)
        acc = None
        for k in range(N_FF_CHUNKS):
            cols = slice(k * ff_chunk, (k + 1) * ff_chunk)
            a = jnp.maximum(
                jnp.dot(h2_ref[...], w_up_ref[:, cols], preferred_element_type=f32), 0.0)
            a = (a * a * rr2).astype(bf16)
            yield
            if acc is None:
                acc = x1_ref[...]
            acc = acc + jnp.dot(a, w_down_ref[cols, :], preferred_element_type=f32)
            if k == N_FF_CHUNKS - 1:
                pre_ref[...] = acc
                sq = acc * acc
                ssq_ref[...] = sum(sq[:, j:j + LANES] for j in range(0, sq.shape[1], LANES))
            yield

    def stage_c():
        pre = pre_ref[...]
        ms = jnp.sum(ssq_ref[...], axis=-1, keepdims=True) * (1.0 / pre.shape[1])
        o_ref[rows, :] = pre * lax.rsqrt(ms + EPS) * gf_ref[...]

    def run(order, after_group=None):
        gens = {}
        for name in order:
            if name == "c":
                stage_c()
                continue
            if name not in gens:
                gens[name] = {"a": stage_a, "b": stage_b}[name]()
            next(gens[name])
            if after_group is not None:
                after_group()
        for gen in gens.values():
            assert next(gen, "done") == "done"

    def first_pass():
        _load_weight_as_bf16(w_in_hbm, w_in_ref)
        _load_weight_as_bf16(w_out_hbm, w_out_ref)
        prepare()

        def mix_while_loading(up_stage, up_sem, down_stage, down_sem):
            streams = (_WeightStream(w_up_hbm, w_up_ref, up_stage, up_sem),
                       _WeightStream(w_down_hbm, w_down_ref, down_stage, down_sem))
            for stream in streams:
                stream.start()

            def drain_some():
                for stream in streams:
                    stream.advance(OVERLAPPED_CHUNKS_PER_GROUP)

            run("aaaaaaa", after_group=drain_some)
            for stream in streams:
                stream.finish()

        pl.run_scoped(mix_while_loading,
                      *_WeightStream.staging(w_up_hbm, OVERLAPPED_LOAD_SLOTS),
                      *_WeightStream.staging(w_down_hbm, OVERLAPPED_LOAD_SLOTS))

    def tile_pass(j, carry):
        nonlocal step, rows
        step = pl.program_id(0) * TILES_PER_STEP + j
        rows = pl.ds(pl.multiple_of(j * tile, tile), tile)
        pl.when(step == 0)(first_pass)
        pl.when((step > 0) & (step < n_tiles))(lambda: run("bc" "a" "aba" "ba" "bb" "aa" "bb" "ab"))
        pl.when(step == n_tiles)(lambda: run("bc" "bbbbbbb"))
        pl.when(step == n_tiles + 1)(lambda: run("c"))
        return carry

    lax.fori_loop(0, TILES_PER_STEP, tile_pass, 0)


def _resident(a):
    return pl.BlockSpec(a.shape, lambda i: (0,) * a.ndim, pipeline_mode=pl.Buffered(1))


def kernel(x, norm1_g, w_in, conv_w, gmlp_ln_g, gmlp_ln_b, gmlp_ws, gmlp_bs, out_norm_conv_g,
           out_norm_gmlp_g, w_out, norm2_g, w_up, w_down, final_g):
    depth = w_in.shape[0]
    batch, seq, d_model = x.shape
    c = conv_w.shape[1]
    gw = gmlp_ln_g.shape[1]
    head_dim = gw // GMLP_HEADS
    assert depth == 1, "one fused block per call"
    assert seq % TILE == 0 and TILE % GMLP_BLOCK == 0
    assert head_dim * HEADS_PER_VREG == LANES
    assert w_up.shape[2] % N_FF_CHUNKS == 0
    tiles_per_batch = seq // TILE
    n_tiles = batch * tiles_per_batch
    bf16 = jnp.bfloat16
    row = lambda a: a.reshape(1, -1)
    bias = jnp.repeat(jnp.transpose(gmlp_bs[0]), head_dim, axis=1)

    assert n_tiles % TILES_PER_STEP == 0
    window = (TILES_PER_STEP * TILE, d_model)
    n_windows = n_tiles // TILES_PER_STEP
    x_spec = pl.BlockSpec(window, lambda i: (jnp.minimum(i, n_windows - 1), 0))
    o_spec = pl.BlockSpec(window, lambda i: (jnp.maximum(i - 1, 0), 0))
    in_hbm = pl.BlockSpec(memory_space=pl.ANY)
    operands_and_specs = (
        (x.reshape(batch * seq, d_model), x_spec),
        (row(norm1_g[0]), None), (w_in[0], in_hbm), (jnp.transpose(conv_w[0]), None),
        (row(gmlp_ln_g[0]), None), (row(gmlp_ln_b[0]), None), (gmlp_ws[0], None), (bias, None),
        (row(out_norm_conv_g[0]), None), (row(out_norm_gmlp_g[0]), None), (w_out[0], in_hbm),
        (row(norm2_g[0]), None), (w_up[0], in_hbm), (w_down[0], in_hbm), (row(final_g), None),
    )
    operands = [a for a, _ in operands_and_specs]
    in_specs = [_resident(a) if spec is None else spec for a, spec in operands_and_specs]
    out = pl.pallas_call(
        functools.partial(_block_kernel, n_tiles=n_tiles, tiles_per_batch=tiles_per_batch),
        grid=(n_windows + 1,),
        in_specs=in_specs,
        out_specs=o_spec,
        out_shape=jax.ShapeDtypeStruct((batch * seq, d_model), x.dtype),
        scratch_shapes=[
            pltpu.VMEM(w_in.shape[1:], bf16),
            pltpu.VMEM(w_out.shape[1:], bf16),
            pltpu.VMEM(w_up.shape[1:], bf16),
            pltpu.VMEM(w_down.shape[1:], bf16),
            pltpu.VMEM((GMLP_HEADS // HEADS_PER_VREG, GMLP_BLOCK, HEADS_PER_VREG * GMLP_BLOCK), bf16),
            pltpu.VMEM((TILE + SUBLANES, c), jnp.float32),
            pltpu.VMEM((TILE, gw), jnp.float32),
            pltpu.VMEM((TILE, d_model), jnp.float32),
            pltpu.VMEM((TILE, d_model), bf16),
            pltpu.VMEM((TILE, LANES), jnp.float32),
            pltpu.VMEM((TILE, d_model), jnp.float32),
            pltpu.VMEM((TILE, LANES), jnp.float32),
        ],
        compiler_params=pltpu.CompilerParams(
            dimension_semantics=("arbitrary",),
            vmem_limit_bytes=VMEM_LIMIT_BYTES),
        name="hybrid_block",
    )(*operands)
    return out.reshape(batch, seq, d_model)
```

```python
import functools

import jax
import jax.numpy as jnp
from jax import lax
from jax.experimental import pallas as pl
from jax.experimental.pallas import tpu as pltpu

EPS = 1e-6
CHUNK = 64
GMLP_BLOCK = 128
GMLP_HEADS = 8
LANES = 128
SUBLANES = 8
HEADS_PER_VREG = 2

TILE = 512
TILES_PER_STEP = 2
N_FF_CHUNKS = 4
VMEM_LIMIT_BYTES = 60000 * 1024

WEIGHT_CHUNK_BYTES = 1024 * 1024
WEIGHT_LOAD_SLOTS = 6
OVERLAPPED_LOAD_SLOTS = 3
OVERLAPPED_CHUNKS_PER_GROUP = 2


def _inv_rms(x):
    return lax.rsqrt(jnp.mean(x * x, axis=-1, keepdims=True) + EPS)


def _rms(x, g):
    return x * _inv_rms(x) * g


class _WeightStream:
    def __init__(self, src_hbm, dst_ref, stage_ref, sem_ref):
        self.src_hbm, self.dst_ref, self.stage_ref, self.sem_ref = src_hbm, dst_ref, stage_ref, sem_ref
        self.slots, self.rows, _ = stage_ref.shape
        assert src_hbm.shape[0] % self.rows == 0
        self.n_chunks = src_hbm.shape[0] // self.rows
        self.done = 0

    @staticmethod
    def staging(src_hbm, slots):
        n_cols = src_hbm.shape[1]
        rows = 1 << ((WEIGHT_CHUNK_BYTES // (n_cols * 4)).bit_length() - 1)
        return (pltpu.VMEM((slots, rows, n_cols), jnp.float32), pltpu.SemaphoreType.DMA((slots,)))

    def _copy(self, i, slot):
        return pltpu.make_async_copy(self.src_hbm.at[pl.ds(i * self.rows, self.rows), :],
                                     self.stage_ref.at[slot], self.sem_ref.at[slot])

    def start(self):
        for i in range(min(self.slots - 1, self.n_chunks)):
            self._copy(i, i).start()

    def advance(self, count):
        first, last = self.done, min(self.done + count, self.n_chunks)

        def step(i, carry):
            ahead = i + self.slots - 1

            @pl.when(ahead < self.n_chunks)
            def _():
                self._copy(ahead, ahead % self.slots).start()

            slot = i % self.slots
            self._copy(i, slot).wait()
            self.dst_ref[pl.ds(pl.multiple_of(i * self.rows, self.rows), self.rows), :] = (
                self.stage_ref[slot].astype(jnp.bfloat16))
            return carry

        lax.fori_loop(first, last, step, 0)
        self.done = last

    def finish(self):
        self.advance(self.n_chunks - self.done)


def _load_weight_as_bf16(src_hbm, dst_ref):
    def body(stage_ref, sem_ref):
        stream = _WeightStream(src_hbm, dst_ref, stage_ref, sem_ref)
        stream.start()
        stream.finish()

    pl.run_scoped(body, *_WeightStream.staging(src_hbm, WEIGHT_LOAD_SLOTS))


def _block_kernel(x_ref, g1_ref, w_in_hbm, conv_w_ref, ln_g_ref, ln_b_ref, ws_ref, bias_ref,
                  gc_ref, gg_ref, w_out_hbm, g2_ref, w_up_hbm, w_down_hbm, gf_ref,
                  o_ref, w_in_ref, w_out_ref, w_up_ref, w_down_ref,
                  ws_pair_ref, zbuf_ref, yb_ref, x1_ref, h2_ref, rr2_ref, pre_ref, ssq_ref,
                  *, n_tiles, tiles_per_batch):
    f32, bf16 = jnp.float32, jnp.bfloat16
    tile = TILE
    c = conv_w_ref.shape[1]
    gw = ln_g_ref.shape[1]
    ff_chunk = w_up_ref.shape[1] // N_FF_CHUNKS
    n_blocks = tile // GMLP_BLOCK
    n_pairs = GMLP_HEADS // HEADS_PER_VREG
    step = rows = None

    def prepare():
        qi = lax.broadcasted_iota(jnp.int32, (GMLP_BLOCK, GMLP_BLOCK), 0) // CHUNK
        kj = lax.broadcasted_iota(jnp.int32, (GMLP_BLOCK, GMLP_BLOCK), 1) // CHUNK
        causal = kj <= qi
        for h in range(GMLP_HEADS):
            g, r = divmod(h, HEADS_PER_VREG)
            ws_pair_ref[g, :, r * GMLP_BLOCK:(r + 1) * GMLP_BLOCK] = (
                jnp.where(causal, ws_ref[h], 0.0).astype(bf16))
        zbuf_ref[0:SUBLANES, :] = jnp.zeros((SUBLANES, c), f32)
        pre_ref[...] = jnp.zeros(pre_ref.shape, f32)
        ssq_ref[...] = jnp.zeros(ssq_ref.shape, f32)

    def stage_a():
        x = x_ref[rows, :]
        h = _rms(x, g1_ref[...]).astype(bf16)
        yield

        def in_proj(lo, width):
            return jnp.dot(h, w_in_ref[:, lo:lo + width], preferred_element_type=f32)

        z = in_proj(c, c) * in_proj(2 * c, c)
        zbuf_ref[SUBLANES:SUBLANES + tile, :] = z
        cw = conv_w_ref[...]
        conv = (zbuf_ref[SUBLANES - 2:SUBLANES - 2 + tile, :] * cw[0:1, :]
                + zbuf_ref[SUBLANES - 1:SUBLANES - 1 + tile, :] * cw[1:2, :]
                + z * cw[2:3, :])
        next_starts_sequence = (step + 1) % tiles_per_batch == 0
        zbuf_ref[0:SUBLANES, :] = jnp.where(next_starts_sequence, 0.0,
                                            zbuf_ref[tile:tile + SUBLANES, :])
        yield
        y_a = _rms(in_proj(0, c) * conv, gc_ref[...]).astype(bf16)
        yield

        v = in_proj(3 * c + gw, gw)
        mu = jnp.mean(v, axis=-1, keepdims=True)
        vc = v - mu
        vn = vc * lax.rsqrt(jnp.mean(vc * vc, axis=-1, keepdims=True) + EPS)
        vn = vn * ln_g_ref[...] + ln_b_ref[...]
        yield
        lane = lax.broadcasted_iota(jnp.int32, (tile, LANES), 1)
        low_head = lane < (LANES // HEADS_PER_VREG)
        for g in range(n_pairs):
            vg = vn[:, g * LANES:(g + 1) * LANES]
            v_lo = jnp.where(low_head, vg, 0.0).astype(bf16)
            v_hi = jnp.where(low_head, 0.0, vg).astype(bf16)
            rhs = jnp.concatenate(
                [jnp.concatenate([v_lo[n * GMLP_BLOCK:(n + 1) * GMLP_BLOCK, :]
                                  for n in range(n_blocks)], axis=1),
                 jnp.concatenate([v_hi[n * GMLP_BLOCK:(n + 1) * GMLP_BLOCK, :]
                                  for n in range(n_blocks)], axis=1)],
                axis=0)
            mixed = jnp.dot(ws_pair_ref[g], rhs, preferred_element_type=f32)
            for n in range(n_blocks):
                yb_ref[n * GMLP_BLOCK:(n + 1) * GMLP_BLOCK, g * LANES:(g + 1) * LANES] = (
                    mixed[:, n * LANES:(n + 1) * LANES])
        yield
        u = in_proj(3 * c, gw)
        bias = jnp.concatenate([bias_ref[...]] * n_blocks, axis=0)
        y_b = _rms(u * (yb_ref[...] + bias), gg_ref[...]).astype(bf16)
        yield

        y = jnp.concatenate([y_a, y_b], axis=-1)
        x1 = x + jnp.dot(y, w_out_ref[...], preferred_element_type=f32)
        x1_ref[...] = x1
        h2_ref[...] = (x1 * g2_ref[...]).astype(bf16)
        r2 = _inv_rms(x1)
        rr2_ref[...] = jnp.broadcast_to(r2 * r2, rr2_ref.shape)
        yield

    def stage_b():
        rr2 = jnp.concatenate([rr2_ref[...]] * (ff_chunk // LANES), axis=1)
        acc = None
        for k in range(N_FF_CHUNKS):
            cols = slice(k * ff_chunk, (k + 1) * ff_chunk)
            a = jnp.maximum(
                jnp.dot(h2_ref[...], w_up_ref[:, cols], preferred_element_type=f32), 0.0)
            a = (a * a * rr2).astype(bf16)
            yield
            if acc is None:
                acc = x1_ref[...]
            acc = acc + jnp.dot(a, w_down_ref[cols, :], preferred_element_type=f32)
            if k == N_FF_CHUNKS - 1:
                pre_ref[...] = acc
                sq = acc * acc
                ssq_ref[...] = sum(sq[:, j:j + LANES] for j in range(0, sq.shape[1], LANES))
            yield

    def stage_c():
        pre = pre_ref[...]
        ms = jnp.sum(ssq_ref[...], axis=-1, keepdims=True) * (1.0 / pre.shape[1])
        o_ref[rows, :] = pre * lax.rsqrt(ms + EPS) * gf_ref[...]

    def run(order, after_group=None):
        gens = {}
        for name in order:
            if name == "c":
                stage_c()
                continue
            if name not in gens:
                gens[name] = {"a": stage_a, "b": stage_b}[name]()
            next(gens[name])
            if after_group is not None:
                after_group()
        for gen in gens.values():
            assert next(gen, "done") == "done"

    def first_pass():
        _load_weight_as_bf16(w_in_hbm, w_in_ref)
        _load_weight_as_bf16(w_out_hbm, w_out_ref)
        prepare()

        def mix_while_loading(up_stage, up_sem, down_stage, down_sem):
            streams = (_WeightStream(w_up_hbm, w_up_ref, up_stage, up_sem),
                       _WeightStream(w_down_hbm, w_down_ref, down_stage, down_sem))
            for stream in streams:
                stream.start()

            def drain_some():
                for stream in streams:
                    stream.advance(OVERLAPPED_CHUNKS_PER_GROUP)

            run("aaaaaaa", after_group=drain_some)
            for stream in streams:
                stream.finish()

        pl.run_scoped(mix_while_loading,
                      *_WeightStream.staging(w_up_hbm, OVERLAPPED_LOAD_SLOTS),
                      *_WeightStream.staging(w_down_hbm, OVERLAPPED_LOAD_SLOTS))

    def tile_pass(j, carry):
        nonlocal step, rows
        step = pl.program_id(0) * TILES_PER_STEP + j
        rows = pl.ds(pl.multiple_of(j * tile, tile), tile)
        steady = (step > 0) & (step < n_tiles)
        pl.when(steady)(lambda: run("bc" "a" "aba" "ba" "bb" "aa" "bb" "ab"))

        @pl.when(jnp.logical_not(steady))
        def _():
            pl.when(step == 0)(first_pass)
            pl.when(step == n_tiles)(lambda: run("bc" "bbbbbbb"))
            pl.when(step == n_tiles + 1)(lambda: run("c"))

        return carry

    lax.fori_loop(0, TILES_PER_STEP, tile_pass, 0)


def _resident(a):
    return pl.BlockSpec(a.shape, lambda i: (0,) * a.ndim, pipeline_mode=pl.Buffered(1))


def kernel(x, norm1_g, w_in, conv_w, gmlp_ln_g, gmlp_ln_b, gmlp_ws, gmlp_bs, out_norm_conv_g,
           out_norm_gmlp_g, w_out, norm2_g, w_up, w_down, final_g):
    depth = w_in.shape[0]
    batch, seq, d_model = x.shape
    c = conv_w.shape[1]
    gw = gmlp_ln_g.shape[1]
    head_dim = gw // GMLP_HEADS
    assert depth == 1, "one fused block per call"
    assert seq % TILE == 0 and TILE % GMLP_BLOCK == 0
    assert head_dim * HEADS_PER_VREG == LANES
    assert w_up.shape[2] % N_FF_CHUNKS == 0
    tiles_per_batch = seq // TILE
    n_tiles = batch * tiles_per_batch
    bf16 = jnp.bfloat16
    row = lambda a: a.reshape(1, -1)
    bias = jnp.repeat(jnp.transpose(gmlp_bs[0]), head_dim, axis=1)

    assert n_tiles % TILES_PER_STEP == 0
    window = (TILES_PER_STEP * TILE, d_model)
    n_windows = n_tiles // TILES_PER_STEP
    x_spec = pl.BlockSpec(window, lambda i: (jnp.minimum(i, n_windows - 1), 0))
    o_spec = pl.BlockSpec(window, lambda i: (jnp.maximum(i - 1, 0), 0))
    in_hbm = pl.BlockSpec(memory_space=pl.ANY)
    operands_and_specs = (
        (x.reshape(batch * seq, d_model), x_spec),
        (row(norm1_g[0]), None), (w_in[0], in_hbm), (jnp.transpose(conv_w[0]), None),
        (row(gmlp_ln_g[0]), None), (row(gmlp_ln_b[0]), None), (gmlp_ws[0], None), (bias, None),
        (row(out_norm_conv_g[0]), None), (row(out_norm_gmlp_g[0]), None), (w_out[0], in_hbm),
        (row(norm2_g[0]), None), (w_up[0], in_hbm), (w_down[0], in_hbm), (row(final_g), None),
    )
    operands = [a for a, _ in operands_and_specs]
    in_specs = [_resident(a) if spec is None else spec for a, spec in operands_and_specs]
    out = pl.pallas_call(
        functools.partial(_block_kernel, n_tiles=n_tiles, tiles_per_batch=tiles_per_batch),
        grid=(n_windows + 1,),
        in_specs=in_specs,
        out_specs=o_spec,
        out_shape=jax.ShapeDtypeStruct((batch * seq, d_model), x.dtype),
        scratch_shapes=[
            pltpu.VMEM(w_in.shape[1:], bf16),
            pltpu.VMEM(w_out.shape[1:], bf16),
            pltpu.VMEM(w_up.shape[1:], bf16),
            pltpu.VMEM(w_down.shape[1:], bf16),
            pltpu.VMEM((GMLP_HEADS // HEADS_PER_VREG, GMLP_BLOCK, HEADS_PER_VREG * GMLP_BLOCK), bf16),
            pltpu.VMEM((TILE + SUBLANES, c), jnp.float32),
            pltpu.VMEM((TILE, gw), jnp.float32),
            pltpu.VMEM((TILE, d_model), jnp.float32),
            pltpu.VMEM((TILE, d_model), bf16),
            pltpu.VMEM((TILE, LANES), jnp.float32),
            pltpu.VMEM((TILE, d_model), jnp.float32),
            pltpu.VMEM((TILE, LANES), jnp.float32),
        ],
        compiler_params=pltpu.CompilerParams(
            dimension_semantics=("arbitrary",),
            vmem_limit_bytes=VMEM_LIMIT_BYTES),
        name="hybrid_block",
    )(*operands)
    return out.reshape(batch, seq, d_model)
```

```python
import functools

import jax
import jax.numpy as jnp
from jax import lax
from jax.experimental import pallas as pl
from jax.experimental.pallas import tpu as pltpu

EPS = 1e-6
CHUNK = 64
GMLP_BLOCK = 128
GMLP_HEADS = 8
LANES = 128
SUBLANES = 8
HEADS_PER_VREG = 2

TILE = 512
N_FF_CHUNKS = 4
VMEM_LIMIT_BYTES = 56 * 1024 * 1024

WEIGHT_CHUNK_BYTES = 1024 * 1024
WEIGHT_LOAD_SLOTS = 8
OVERLAPPED_LOAD_SLOTS = 4
OVERLAPPED_CHUNKS_PER_GROUP = 2


def _inv_rms(x):
    return lax.rsqrt(jnp.mean(x * x, axis=-1, keepdims=True) + EPS)


def _rms(x, g):
    return x * _inv_rms(x) * g


class _WeightStream:
    def __init__(self, src_hbm, dst_ref, stage_ref, sem_ref):
        self.src_hbm, self.dst_ref, self.stage_ref, self.sem_ref = src_hbm, dst_ref, stage_ref, sem_ref
        self.slots, self.rows, _ = stage_ref.shape
        assert src_hbm.shape[0] % self.rows == 0
        self.n_chunks = src_hbm.shape[0] // self.rows
        self.done = 0

    @staticmethod
    def staging(src_hbm, slots):
        n_cols = src_hbm.shape[1]
        rows = 1 << ((WEIGHT_CHUNK_BYTES // (n_cols * 4)).bit_length() - 1)
        return (pltpu.VMEM((slots, rows, n_cols), jnp.float32), pltpu.SemaphoreType.DMA((slots,)))

    def _copy(self, i, slot):
        return pltpu.make_async_copy(self.src_hbm.at[pl.ds(i * self.rows, self.rows), :],
                                     self.stage_ref.at[slot], self.sem_ref.at[slot])

    def start(self):
        for i in range(min(self.slots - 1, self.n_chunks)):
            self._copy(i, i).start()

    def advance(self, count):
        first, last = self.done, min(self.done + count, self.n_chunks)

        def step(i, carry):
            ahead = i + self.slots - 1

            @pl.when(ahead < self.n_chunks)
            def _():
                self._copy(ahead, ahead % self.slots).start()

            slot = i % self.slots
            self._copy(i, slot).wait()
            self.dst_ref[pl.ds(pl.multiple_of(i * self.rows, self.rows), self.rows), :] = (
                self.stage_ref[slot].astype(jnp.bfloat16))
            return carry

        lax.fori_loop(first, last, step, 0)
        self.done = last

    def finish(self):
        self.advance(self.n_chunks - self.done)


def _load_weight_as_bf16(src_hbm, dst_ref):
    def body(stage_ref, sem_ref):
        stream = _WeightStream(src_hbm, dst_ref, stage_ref, sem_ref)
        stream.start()
        stream.finish()

    pl.run_scoped(body, *_WeightStream.staging(src_hbm, WEIGHT_LOAD_SLOTS))


def _block_kernel(x_ref, g1_ref, w_in_hbm, conv_w_ref, ln_g_ref, ln_b_ref, ws_ref, bias_ref,
                  gc_ref, gg_ref, w_out_hbm, g2_ref, w_up_hbm, w_down_hbm, gf_ref,
                  o_ref, w_in_ref, w_out_ref, w_up_ref, w_down_ref,
                  ws_pair_ref, zbuf_ref, yb_ref, x1_ref, h2_ref, rr2_ref, pre_ref, ssq_ref,
                  *, n_tiles, tiles_per_batch):
    f32, bf16 = jnp.float32, jnp.bfloat16
    tile = x_ref.shape[0]
    c = conv_w_ref.shape[1]
    gw = ln_g_ref.shape[1]
    ff_chunk = w_up_ref.shape[1] // N_FF_CHUNKS
    n_blocks = tile // GMLP_BLOCK
    n_pairs = GMLP_HEADS // HEADS_PER_VREG
    step = pl.program_id(0)

    def prepare():
        qi = lax.broadcasted_iota(jnp.int32, (GMLP_BLOCK, GMLP_BLOCK), 0) // CHUNK
        kj = lax.broadcasted_iota(jnp.int32, (GMLP_BLOCK, GMLP_BLOCK), 1) // CHUNK
        causal = kj <= qi
        for h in range(GMLP_HEADS):
            g, r = divmod(h, HEADS_PER_VREG)
            ws_pair_ref[g, :, r * GMLP_BLOCK:(r + 1) * GMLP_BLOCK] = (
                jnp.where(causal, ws_ref[h], 0.0).astype(bf16))
        zbuf_ref[0:SUBLANES, :] = jnp.zeros((SUBLANES, c), f32)
        pre_ref[...] = jnp.zeros(pre_ref.shape, f32)
        ssq_ref[...] = jnp.zeros(ssq_ref.shape, f32)

    def stage_a():
        x = x_ref[...]
        h = _rms(x, g1_ref[...]).astype(bf16)
        yield

        def in_proj(lo, width):
            return jnp.dot(h, w_in_ref[:, lo:lo + width], preferred_element_type=f32)

        z = in_proj(c, c) * in_proj(2 * c, c)
        zbuf_ref[SUBLANES:SUBLANES + tile, :] = z
        cw = conv_w_ref[...]
        conv = (zbuf_ref[SUBLANES - 2:SUBLANES - 2 + tile, :] * cw[0:1, :]
                + zbuf_ref[SUBLANES - 1:SUBLANES - 1 + tile, :] * cw[1:2, :]
                + z * cw[2:3, :])
        next_starts_sequence = (step + 1) % tiles_per_batch == 0
        zbuf_ref[0:SUBLANES, :] = jnp.where(next_starts_sequence, 0.0,
                                            zbuf_ref[tile:tile + SUBLANES, :])
        yield
        y_a = _rms(in_proj(0, c) * conv, gc_ref[...]).astype(bf16)
        yield

        v = in_proj(3 * c + gw, gw)
        mu = jnp.mean(v, axis=-1, keepdims=True)
        vc = v - mu
        vn = vc * lax.rsqrt(jnp.mean(vc * vc, axis=-1, keepdims=True) + EPS)
        vn = vn * ln_g_ref[...] + ln_b_ref[...]
        yield
        lane = lax.broadcasted_iota(jnp.int32, (tile, LANES), 1)
        low_head = lane < (LANES // HEADS_PER_VREG)
        for g in range(n_pairs):
            vg = vn[:, g * LANES:(g + 1) * LANES]
            v_lo = jnp.where(low_head, vg, 0.0).astype(bf16)
            v_hi = jnp.where(low_head, 0.0, vg).astype(bf16)
            rhs = jnp.concatenate(
                [jnp.concatenate([v_lo[n * GMLP_BLOCK:(n + 1) * GMLP_BLOCK, :]
                                  for n in range(n_blocks)], axis=1),
                 jnp.concatenate([v_hi[n * GMLP_BLOCK:(n + 1) * GMLP_BLOCK, :]
                                  for n in range(n_blocks)], axis=1)],
                axis=0)
            mixed = jnp.dot(ws_pair_ref[g], rhs, preferred_element_type=f32)
            for n in range(n_blocks):
                yb_ref[n * GMLP_BLOCK:(n + 1) * GMLP_BLOCK, g * LANES:(g + 1) * LANES] = (
                    mixed[:, n * LANES:(n + 1) * LANES])
        yield
        u = in_proj(3 * c, gw)
        bias = jnp.concatenate([bias_ref[...]] * n_blocks, axis=0)
        y_b = _rms(u * (yb_ref[...] + bias), gg_ref[...]).astype(bf16)
        yield

        y = jnp.concatenate([y_a, y_b], axis=-1)
        x1 = x + jnp.dot(y, w_out_ref[...], preferred_element_type=f32)
        x1_ref[...] = x1
        h2_ref[...] = (x1 * g2_ref[...]).astype(bf16)
        r2 = _inv_rms(x1)
        rr2_ref[...] = jnp.broadcast_to(r2 * r2, rr2_ref.shape)
        yield

    def stage_b():
        rr2 = jnp.concatenate([rr2_ref[...]] * (ff_chunk // LANES), axis=1)
        acc = None
        for k in range(N_FF_CHUNKS):
            cols = slice(k * ff_chunk, (k + 1) * ff_chunk)
            a = jnp.maximum(
                jnp.dot(h2_ref[...], w_up_ref[:, cols], preferred_element_type=f32), 0.0)
            a = (a * a * rr2).astype(bf16)
            yield
            if acc is None:
                acc = x1_ref[...]
            acc = acc + jnp.dot(a, w_down_ref[cols, :], preferred_element_type=f32)
            if k == N_FF_CHUNKS - 1:
                pre_ref[...] = acc
                sq = acc * acc
                ssq_ref[...] = sum(sq[:, j:j + LANES] for j in range(0, sq.shape[1], LANES))
            yield

    def stage_c():
        pre = pre_ref[...]
        ms = jnp.sum(ssq_ref[...], axis=-1, keepdims=True) * (1.0 / pre.shape[1])
        o_ref[...] = pre * lax.rsqrt(ms + EPS) * gf_ref[...]

    def run(order, after_group=None):
        gens = {}
        for name in order:
            if name == "c":
                stage_c()
                continue
            if name not in gens:
                gens[name] = {"a": stage_a, "b": stage_b}[name]()
            next(gens[name])
            if after_group is not None:
                after_group()
        for gen in gens.values():
            assert next(gen, "done") == "done"

    def first_step():
        _load_weight_as_bf16(w_in_hbm, w_in_ref)
        _load_weight_as_bf16(w_out_hbm, w_out_ref)
        prepare()

        def mix_while_loading(up_stage, up_sem, down_stage, down_sem):
            streams = (_WeightStream(w_up_hbm, w_up_ref, up_stage, up_sem),
                       _WeightStream(w_down_hbm, w_down_ref, down_stage, down_sem))
            for stream in streams:
                stream.start()

            def drain_some():
                for stream in streams:
                    stream.advance(OVERLAPPED_CHUNKS_PER_GROUP)

            run("aaaaaaa", after_group=drain_some)
            for stream in streams:
                stream.finish()

        pl.run_scoped(mix_while_loading,
                      *_WeightStream.staging(w_up_hbm, OVERLAPPED_LOAD_SLOTS),
                      *_WeightStream.staging(w_down_hbm, OVERLAPPED_LOAD_SLOTS))

    steady = (step > 0) & (step < n_tiles)

    @pl.when(steady)
    def _():
        run("bc" "a" "aba" "ba" "bb" "aa" "bb" "ab")

    @pl.when(jnp.logical_not(steady))
    def _():
        pl.when(step == 0)(first_step)
        pl.when(step == n_tiles)(lambda: run("bc" "bbbbbbb"))
        pl.when(step == n_tiles + 1)(lambda: run("c"))


def _resident(a):
    return pl.BlockSpec(a.shape, lambda i: (0,) * a.ndim, pipeline_mode=pl.Buffered(1))


def kernel(x, norm1_g, w_in, conv_w, gmlp_ln_g, gmlp_ln_b, gmlp_ws, gmlp_bs, out_norm_conv_g,
           out_norm_gmlp_g, w_out, norm2_g, w_up, w_down, final_g):
    depth = w_in.shape[0]
    batch, seq, d_model = x.shape
    c = conv_w.shape[1]
    gw = gmlp_ln_g.shape[1]
    head_dim = gw // GMLP_HEADS
    assert depth == 1, "one fused block per call"
    assert seq % TILE == 0 and TILE % GMLP_BLOCK == 0
    assert head_dim * HEADS_PER_VREG == LANES
    assert w_up.shape[2] % N_FF_CHUNKS == 0
    tiles_per_batch = seq // TILE
    n_tiles = batch * tiles_per_batch
    bf16 = jnp.bfloat16
    row = lambda a: a.reshape(1, -1)
    bias = jnp.repeat(jnp.transpose(gmlp_bs[0]), head_dim, axis=1)

    x_spec = pl.BlockSpec((TILE, d_model), lambda i: (jnp.minimum(i, n_tiles - 1), 0))
    o_spec = pl.BlockSpec((TILE, d_model), lambda i: (jnp.maximum(i - 2, 0), 0))
    in_hbm = pl.BlockSpec(memory_space=pl.ANY)
    operands_and_specs = (
        (x.reshape(batch * seq, d_model), x_spec),
        (row(norm1_g[0]), None), (w_in[0], in_hbm), (jnp.transpose(conv_w[0]), None),
        (row(gmlp_ln_g[0]), None), (row(gmlp_ln_b[0]), None), (gmlp_ws[0], None), (bias, None),
        (row(out_norm_conv_g[0]), None), (row(out_norm_gmlp_g[0]), None), (w_out[0], in_hbm),
        (row(norm2_g[0]), None), (w_up[0], in_hbm), (w_down[0], in_hbm), (row(final_g), None),
    )
    operands = [a for a, _ in operands_and_specs]
    in_specs = [_resident(a) if spec is None else spec for a, spec in operands_and_specs]
    out = pl.pallas_call(
        functools.partial(_block_kernel, n_tiles=n_tiles, tiles_per_batch=tiles_per_batch),
        grid=(n_tiles + 2,),
        in_specs=in_specs,
        out_specs=o_spec,
        out_shape=jax.ShapeDtypeStruct((batch * seq, d_model), x.dtype),
        scratch_shapes=[
            pltpu.VMEM(w_in.shape[1:], bf16),
            pltpu.VMEM(w_out.shape[1:], bf16),
            pltpu.VMEM(w_up.shape[1:], bf16),
            pltpu.VMEM(w_down.shape[1:], bf16),
            pltpu.VMEM((GMLP_HEADS // HEADS_PER_VREG, GMLP_BLOCK, HEADS_PER_VREG * GMLP_BLOCK), bf16),
            pltpu.VMEM((TILE + SUBLANES, c), jnp.float32),
            pltpu.VMEM((TILE, gw), jnp.float32),
            pltpu.VMEM((TILE, d_model), jnp.float32),
            pltpu.VMEM((TILE, d_model), bf16),
            pltpu.VMEM((TILE, LANES), jnp.float32),
            pltpu.VMEM((TILE, d_model), jnp.float32),
            pltpu.VMEM((TILE, LANES), jnp.float32),
        ],
        compiler_params=pltpu.CompilerParams(
            dimension_semantics=("arbitrary",),
            vmem_limit_bytes=VMEM_LIMIT_BYTES),
        name="hybrid_block",
    )(*operands)
    return out.reshape(batch, seq, d_model)
```

```python
import functools

import jax
import jax.numpy as jnp
from jax import lax
from jax.experimental import pallas as pl
from jax.experimental.pallas import tpu as pltpu

EPS = 1e-6
CHUNK = 64
GMLP_BLOCK = 128
GMLP_HEADS = 8
LANES = 128
SUBLANES = 8
HEADS_PER_VREG = 2

TILE = 512
N_FF_CHUNKS = 4
VMEM_LIMIT_BYTES = 56 * 1024 * 1024

WEIGHT_CHUNK_BYTES = 1024 * 1024
WEIGHT_LOAD_SLOTS = 8
OVERLAPPED_LOAD_SLOTS = 4
OVERLAPPED_CHUNKS_PER_GROUP = 2


def _inv_rms(x):
    return lax.rsqrt(jnp.mean(x * x, axis=-1, keepdims=True) + EPS)


def _rms(x, g):
    return x * _inv_rms(x) * g


class _WeightStream:
    def __init__(self, src_hbm, dst_ref, stage_ref, sem_ref):
        self.src_hbm, self.dst_ref, self.stage_ref, self.sem_ref = src_hbm, dst_ref, stage_ref, sem_ref
        self.slots, self.rows, _ = stage_ref.shape
        assert src_hbm.shape[0] % self.rows == 0
        self.n_chunks = src_hbm.shape[0] // self.rows
        self.done = 0

    @staticmethod
    def staging(src_hbm, slots):
        n_cols = src_hbm.shape[1]
        rows = 1 << ((WEIGHT_CHUNK_BYTES // (n_cols * 4)).bit_length() - 1)
        return (pltpu.VMEM((slots, rows, n_cols), jnp.float32), pltpu.SemaphoreType.DMA((slots,)))

    def _copy(self, i, slot):
        return pltpu.make_async_copy(self.src_hbm.at[pl.ds(i * self.rows, self.rows), :],
                                     self.stage_ref.at[slot], self.sem_ref.at[slot])

    def _start(self, i, slot):
        if isinstance(i, int):
            self._copy(i, slot).start(priority=i % 2)
            return
        for priority in (0, 1):
            pl.when(i % 2 == priority)(
                functools.partial(self._copy(i, slot).start, priority=priority))

    def start(self):
        for i in range(min(self.slots - 1, self.n_chunks)):
            self._start(i, i)

    def advance(self, count):
        first, last = self.done, min(self.done + count, self.n_chunks)

        def step(i, carry):
            ahead = i + self.slots - 1

            @pl.when(ahead < self.n_chunks)
            def _():
                self._start(ahead, ahead % self.slots)

            slot = i % self.slots
            self._copy(i, slot).wait()
            self.dst_ref[pl.ds(pl.multiple_of(i * self.rows, self.rows), self.rows), :] = (
                self.stage_ref[slot].astype(jnp.bfloat16))
            return carry

        lax.fori_loop(first, last, step, 0)
        self.done = last

    def finish(self):
        self.advance(self.n_chunks - self.done)


def _load_weight_as_bf16(src_hbm, dst_ref):
    def body(stage_ref, sem_ref):
        stream = _WeightStream(src_hbm, dst_ref, stage_ref, sem_ref)
        stream.start()
        stream.finish()

    pl.run_scoped(body, *_WeightStream.staging(src_hbm, WEIGHT_LOAD_SLOTS))


def _block_kernel(x_hbm, g1_ref, w_in_hbm, conv_w_ref, ln_g_ref, ln_b_ref, ws_ref, bias_ref,
                  gc_ref, gg_ref, w_out_hbm, g2_ref, w_up_hbm, w_down_hbm, gf_ref,
                  out_hbm, w_in_ref, w_out_ref, w_up_ref, w_down_ref,
                  ws_pair_ref, zbuf_ref, yb_ref, x1_ref, h2_ref, rr2_ref, pre_ref, ssq_ref,
                  x_buf, o_buf, x_sem, o_sem, *, n_tiles, tiles_per_batch):
    f32, bf16 = jnp.float32, jnp.bfloat16
    tile = TILE
    c = conv_w_ref.shape[1]
    gw = ln_g_ref.shape[1]
    ff_chunk = w_up_ref.shape[1] // N_FF_CHUNKS
    n_blocks = tile // GMLP_BLOCK
    n_pairs = GMLP_HEADS // HEADS_PER_VREG
    step = slot = None

    def prepare():
        qi = lax.broadcasted_iota(jnp.int32, (GMLP_BLOCK, GMLP_BLOCK), 0) // CHUNK
        kj = lax.broadcasted_iota(jnp.int32, (GMLP_BLOCK, GMLP_BLOCK), 1) // CHUNK
        causal = kj <= qi
        for h in range(GMLP_HEADS):
            g, r = divmod(h, HEADS_PER_VREG)
            ws_pair_ref[g, :, r * GMLP_BLOCK:(r + 1) * GMLP_BLOCK] = (
                jnp.where(causal, ws_ref[h], 0.0).astype(bf16))
        zbuf_ref[0:SUBLANES, :] = jnp.zeros((SUBLANES, c), f32)
        pre_ref[...] = jnp.zeros(pre_ref.shape, f32)
        ssq_ref[...] = jnp.zeros(ssq_ref.shape, f32)

    def stage_a():
        x = x_buf[slot]
        h = _rms(x, g1_ref[...]).astype(bf16)
        yield

        def in_proj(lo, width):
            return jnp.dot(h, w_in_ref[:, lo:lo + width], preferred_element_type=f32)

        z = in_proj(c, c) * in_proj(2 * c, c)
        zbuf_ref[SUBLANES:SUBLANES + tile, :] = z
        cw = conv_w_ref[...]
        conv = (zbuf_ref[SUBLANES - 2:SUBLANES - 2 + tile, :] * cw[0:1, :]
                + zbuf_ref[SUBLANES - 1:SUBLANES - 1 + tile, :] * cw[1:2, :]
                + z * cw[2:3, :])
        next_starts_sequence = (step + 1) % tiles_per_batch == 0
        zbuf_ref[0:SUBLANES, :] = jnp.where(next_starts_sequence, 0.0,
                                            zbuf_ref[tile:tile + SUBLANES, :])
        yield
        y_a = _rms(in_proj(0, c) * conv, gc_ref[...]).astype(bf16)
        yield

        v = in_proj(3 * c + gw, gw)
        mu = jnp.mean(v, axis=-1, keepdims=True)
        vc = v - mu
        vn = vc * lax.rsqrt(jnp.mean(vc * vc, axis=-1, keepdims=True) + EPS)
        vn = vn * ln_g_ref[...] + ln_b_ref[...]
        yield
        lane = lax.broadcasted_iota(jnp.int32, (tile, LANES), 1)
        low_head = lane < (LANES // HEADS_PER_VREG)
        for g in range(n_pairs):
            vg = vn[:, g * LANES:(g + 1) * LANES]
            v_lo = jnp.where(low_head, vg, 0.0).astype(bf16)
            v_hi = jnp.where(low_head, 0.0, vg).astype(bf16)
            rhs = jnp.concatenate(
                [jnp.concatenate([v_lo[n * GMLP_BLOCK:(n + 1) * GMLP_BLOCK, :]
                                  for n in range(n_blocks)], axis=1),
                 jnp.concatenate([v_hi[n * GMLP_BLOCK:(n + 1) * GMLP_BLOCK, :]
                                  for n in range(n_blocks)], axis=1)],
                axis=0)
            mixed = jnp.dot(ws_pair_ref[g], rhs, preferred_element_type=f32)
            for n in range(n_blocks):
                yb_ref[n * GMLP_BLOCK:(n + 1) * GMLP_BLOCK, g * LANES:(g + 1) * LANES] = (
                    mixed[:, n * LANES:(n + 1) * LANES])
        yield
        u = in_proj(3 * c, gw)
        bias = jnp.concatenate([bias_ref[...]] * n_blocks, axis=0)
        y_b = _rms(u * (yb_ref[...] + bias), gg_ref[...]).astype(bf16)
        yield

        y = jnp.concatenate([y_a, y_b], axis=-1)
        x1 = x + jnp.dot(y, w_out_ref[...], preferred_element_type=f32)
        x1_ref[...] = x1
        h2_ref[...] = (x1 * g2_ref[...]).astype(bf16)
        r2 = _inv_rms(x1)
        rr2_ref[...] = jnp.broadcast_to(r2 * r2, rr2_ref.shape)
        yield

    def stage_b():
        rr2 = jnp.concatenate([rr2_ref[...]] * (ff_chunk // LANES), axis=1)
        acc = None
        for k in range(N_FF_CHUNKS):
            cols = slice(k * ff_chunk, (k + 1) * ff_chunk)
            a = jnp.maximum(
                jnp.dot(h2_ref[...], w_up_ref[:, cols], preferred_element_type=f32), 0.0)
            a = (a * a * rr2).astype(bf16)
            yield
            if acc is None:
                acc = x1_ref[...]
            acc = acc + jnp.dot(a, w_down_ref[cols, :], preferred_element_type=f32)
            if k == N_FF_CHUNKS - 1:
                pre_ref[...] = acc
                sq = acc * acc
                ssq_ref[...] = sum(sq[:, j:j + LANES] for j in range(0, sq.shape[1], LANES))
            yield

    def stage_c():
        pre = pre_ref[...]
        ms = jnp.sum(ssq_ref[...], axis=-1, keepdims=True) * (1.0 / pre.shape[1])
        o_buf[slot] = pre * lax.rsqrt(ms + EPS) * gf_ref[...]

    def run(order, after_group=None):
        gens = {}
        for name in order:
            if name == "c":
                stage_c()
                continue
            if name not in gens:
                gens[name] = {"a": stage_a, "b": stage_b}[name]()
            next(gens[name])
            if after_group is not None:
                after_group()
        for gen in gens.values():
            assert next(gen, "done") == "done"

    def first_step():
        _load_weight_as_bf16(w_in_hbm, w_in_ref)
        _load_weight_as_bf16(w_out_hbm, w_out_ref)
        prepare()

        def mix_while_loading(up_stage, up_sem, down_stage, down_sem):
            streams = (_WeightStream(w_up_hbm, w_up_ref, up_stage, up_sem),
                       _WeightStream(w_down_hbm, w_down_ref, down_stage, down_sem))
            for stream in streams:
                stream.start()

            def drain_some():
                for stream in streams:
                    stream.advance(OVERLAPPED_CHUNKS_PER_GROUP)

            run("aaaaaaa", after_group=drain_some)
            for stream in streams:
                stream.finish()

        pl.run_scoped(mix_while_loading,
                      *_WeightStream.staging(w_up_hbm, OVERLAPPED_LOAD_SLOTS),
                      *_WeightStream.staging(w_down_hbm, OVERLAPPED_LOAD_SLOTS))

    def x_copy(t, buf):
        return pltpu.make_async_copy(x_hbm.at[pl.ds(t * tile, tile), :], x_buf.at[buf], x_sem.at[buf])

    def out_copy(t, buf):
        return pltpu.make_async_copy(o_buf.at[buf], out_hbm.at[pl.ds(t * tile, tile), :], o_sem.at[buf])

    assert n_tiles >= 4

    step, slot = 0, 0
    x_copy(0, 0).start()
    x_copy(1, 1).start()
    x_copy(0, 0).wait()
    first_step()

    def steady_pass(t, carry):
        nonlocal step, slot
        step, slot = t, t % 2
        pl.when(t + 1 < n_tiles)(lambda: x_copy(t + 1, 1 - slot).start())
        x_copy(t, slot).wait()
        pl.when(t >= 4)(lambda: out_copy(t - 4, slot).wait())
        run("bc" "a" "aba" "ba" "bb" "aa" "bb" "ab")
        pl.when(t >= 2)(lambda: out_copy(t - 2, slot).start())
        return carry

    lax.fori_loop(1, n_tiles, steady_pass, 0)

    for t, stages in ((n_tiles, "bc" "bbbbbbb"), (n_tiles + 1, "c")):
        step, slot = t, t % 2
        out_copy(t - 4, slot).wait()
        run(stages)
        out_copy(t - 2, slot).start()
    for t in (n_tiles - 2, n_tiles - 1):
        out_copy(t, t % 2).wait()


def _resident(a):
    del a
    return pl.BlockSpec(memory_space=pltpu.VMEM)


def kernel(x, norm1_g, w_in, conv_w, gmlp_ln_g, gmlp_ln_b, gmlp_ws, gmlp_bs, out_norm_conv_g,
           out_norm_gmlp_g, w_out, norm2_g, w_up, w_down, final_g):
    depth = w_in.shape[0]
    batch, seq, d_model = x.shape
    c = conv_w.shape[1]
    gw = gmlp_ln_g.shape[1]
    head_dim = gw // GMLP_HEADS
    assert depth == 1, "one fused block per call"
    assert seq % TILE == 0 and TILE % GMLP_BLOCK == 0
    assert head_dim * HEADS_PER_VREG == LANES
    assert w_up.shape[2] % N_FF_CHUNKS == 0
    tiles_per_batch = seq // TILE
    n_tiles = batch * tiles_per_batch
    bf16 = jnp.bfloat16
    row = lambda a: a.reshape(1, -1)
    bias = jnp.repeat(jnp.transpose(gmlp_bs[0]), head_dim, axis=1)

    in_hbm = pl.BlockSpec(memory_space=pl.ANY)
    operands_and_specs = (
        (x.reshape(batch * seq, d_model), in_hbm),
        (row(norm1_g[0]), None), (w_in[0], in_hbm), (jnp.transpose(conv_w[0]), None),
        (row(gmlp_ln_g[0]), None), (row(gmlp_ln_b[0]), None), (gmlp_ws[0], None), (bias, None),
        (row(out_norm_conv_g[0]), None), (row(out_norm_gmlp_g[0]), None), (w_out[0], in_hbm),
        (row(norm2_g[0]), None), (w_up[0], in_hbm), (w_down[0], in_hbm), (row(final_g), None),
    )
    operands = [a for a, _ in operands_and_specs]
    in_specs = [_resident(a) if spec is None else spec for a, spec in operands_and_specs]
    out = pl.pallas_call(
        functools.partial(_block_kernel, n_tiles=n_tiles, tiles_per_batch=tiles_per_batch),
        in_specs=in_specs,
        out_specs=pl.BlockSpec(memory_space=pl.ANY),
        out_shape=jax.ShapeDtypeStruct((batch * seq, d_model), x.dtype),
        scratch_shapes=[
            pltpu.VMEM(w_in.shape[1:], bf16),
            pltpu.VMEM(w_out.shape[1:], bf16),
            pltpu.VMEM(w_up.shape[1:], bf16),
            pltpu.VMEM(w_down.shape[1:], bf16),
            pltpu.VMEM((GMLP_HEADS // HEADS_PER_VREG, GMLP_BLOCK, HEADS_PER_VREG * GMLP_BLOCK), bf16),
            pltpu.VMEM((TILE + SUBLANES, c), jnp.float32),
            pltpu.VMEM((TILE, gw), jnp.float32),
            pltpu.VMEM((TILE, d_model), jnp.float32),
            pltpu.VMEM((TILE, d_model), bf16),
            pltpu.VMEM((TILE, LANES), jnp.float32),
            pltpu.VMEM((TILE, d_model), jnp.float32),
            pltpu.VMEM((TILE, LANES), jnp.float32),
            pltpu.VMEM((2, TILE, d_model), jnp.float32),
            pltpu.VMEM((2, TILE, d_model), jnp.float32),
            pltpu.SemaphoreType.DMA((2,)),
            pltpu.SemaphoreType.DMA((2,)),
        ],
        compiler_params=pltpu.CompilerParams(vmem_limit_bytes=VMEM_LIMIT_BYTES),
        name="hybrid_block",
    )(*operands)
    return out.reshape(batch, seq, d_model)
```

```python
import functools

import jax
import jax.numpy as jnp
from jax import lax
from jax.experimental import pallas as pl
from jax.experimental.pallas import tpu as pltpu

EPS = 1e-6
CHUNK = 64
GMLP_BLOCK = 128
GMLP_HEADS = 8
LANES = 128
SUBLANES = 8
HEADS_PER_VREG = 2

TILE = 512
N_FF_CHUNKS = 4
VMEM_LIMIT_BYTES = 56 * 1024 * 1024

WEIGHT_CHUNK_BYTES = 1024 * 1024
WEIGHT_LOAD_SLOTS = 8
OVERLAPPED_LOAD_SLOTS = 4
OVERLAPPED_CHUNKS_PER_GROUP = 2


def _inv_rms(x):
    return lax.rsqrt(jnp.mean(x * x, axis=-1, keepdims=True) + EPS)


def _rms(x, g):
    return x * _inv_rms(x) * g


class _WeightStream:
    def __init__(self, src_hbm, dst_ref, stage_ref, sem_ref):
        self.src_hbm, self.dst_ref, self.stage_ref, self.sem_ref = src_hbm, dst_ref, stage_ref, sem_ref
        self.slots, self.rows, _ = stage_ref.shape
        assert src_hbm.shape[0] % self.rows == 0
        self.n_chunks = src_hbm.shape[0] // self.rows
        self.done = 0

    @staticmethod
    def staging(src_hbm, slots):
        n_cols = src_hbm.shape[1]
        rows = 1 << ((WEIGHT_CHUNK_BYTES // (n_cols * 4)).bit_length() - 1)
        return (pltpu.VMEM((slots, rows, n_cols), jnp.float32), pltpu.SemaphoreType.DMA((slots,)))

    def _copy(self, i, slot):
        return pltpu.make_async_copy(self.src_hbm.at[pl.ds(i * self.rows, self.rows), :],
                                     self.stage_ref.at[slot], self.sem_ref.at[slot])

    def start(self):
        for i in range(min(self.slots - 1, self.n_chunks)):
            self._copy(i, i).start()

    def advance(self, count):
        first, last = self.done, min(self.done + count, self.n_chunks)

        def step(i, carry):
            ahead = i + self.slots - 1

            @pl.when(ahead < self.n_chunks)
            def _():
                self._copy(ahead, ahead % self.slots).start()

            slot = i % self.slots
            self._copy(i, slot).wait()
            self.dst_ref[pl.ds(pl.multiple_of(i * self.rows, self.rows), self.rows), :] = (
                self.stage_ref[slot].astype(jnp.bfloat16))
            return carry

        lax.fori_loop(first, last, step, 0)
        self.done = last

    def finish(self):
        self.advance(self.n_chunks - self.done)


def _load_weight_as_bf16(src_hbm, dst_ref):
    def body(stage_ref, sem_ref):
        stream = _WeightStream(src_hbm, dst_ref, stage_ref, sem_ref)
        stream.start()
        stream.finish()

    pl.run_scoped(body, *_WeightStream.staging(src_hbm, WEIGHT_LOAD_SLOTS))


def _block_kernel(x_hbm, g1_ref, w_in_hbm, conv_w_ref, ln_g_ref, ln_b_ref, ws_ref, bias_ref,
                  gc_ref, gg_ref, w_out_hbm, g2_ref, w_up_hbm, w_down_hbm, gf_ref,
                  out_hbm, w_in_ref, w_out_ref, w_up_ref, w_down_ref,
                  ws_pair_ref, zbuf_ref, yb_ref, x1_ref, h2_ref, rr2_ref, pre_ref, ssq_ref,
                  x_buf, o_buf, x_sem, o_sem, *, n_tiles, tiles_per_batch):
    f32, bf16 = jnp.float32, jnp.bfloat16
    tile = TILE
    c = conv_w_ref.shape[1]
    gw = ln_g_ref.shape[1]
    ff_chunk = w_up_ref.shape[1] // N_FF_CHUNKS
    n_blocks = tile // GMLP_BLOCK
    n_pairs = GMLP_HEADS // HEADS_PER_VREG
    step = slot = None

    def prepare():
        qi = lax.broadcasted_iota(jnp.int32, (GMLP_BLOCK, GMLP_BLOCK), 0) // CHUNK
        kj = lax.broadcasted_iota(jnp.int32, (GMLP_BLOCK, GMLP_BLOCK), 1) // CHUNK
        causal = kj <= qi
        for h in range(GMLP_HEADS):
            g, r = divmod(h, HEADS_PER_VREG)
            ws_pair_ref[g, :, r * GMLP_BLOCK:(r + 1) * GMLP_BLOCK] = (
                jnp.where(causal, ws_ref[h], 0.0).astype(bf16))
        zbuf_ref[0:SUBLANES, :] = jnp.zeros((SUBLANES, c), f32)
        pre_ref[...] = jnp.zeros(pre_ref.shape, f32)
        ssq_ref[...] = jnp.zeros(ssq_ref.shape, f32)

    def stage_a():
        x = x_buf[slot]
        h = _rms(x, g1_ref[...]).astype(bf16)
        yield

        def in_proj(lo, width):
            return jnp.dot(h, w_in_ref[:, lo:lo + width], preferred_element_type=f32)

        z = in_proj(c, c) * in_proj(2 * c, c)
        zbuf_ref[SUBLANES:SUBLANES + tile, :] = z
        cw = conv_w_ref[...]
        conv = (zbuf_ref[SUBLANES - 2:SUBLANES - 2 + tile, :] * cw[0:1, :]
                + zbuf_ref[SUBLANES - 1:SUBLANES - 1 + tile, :] * cw[1:2, :]
                + z * cw[2:3, :])
        next_starts_sequence = (step + 1) % tiles_per_batch == 0
        zbuf_ref[0:SUBLANES, :] = jnp.where(next_starts_sequence, 0.0,
                                            zbuf_ref[tile:tile + SUBLANES, :])
        yield
        y_a = _rms(in_proj(0, c) * conv, gc_ref[...]).astype(bf16)
        yield

        v = in_proj(3 * c + gw, gw)
        mu = jnp.mean(v, axis=-1, keepdims=True)
        vc = v - mu
        vn = vc * lax.rsqrt(jnp.mean(vc * vc, axis=-1, keepdims=True) + EPS)
        vn = vn * ln_g_ref[...] + ln_b_ref[...]
        yield
        lane = lax.broadcasted_iota(jnp.int32, (tile, LANES), 1)
        low_head = lane < (LANES // HEADS_PER_VREG)
        for g in range(n_pairs):
            vg = vn[:, g * LANES:(g + 1) * LANES]
            v_lo = jnp.where(low_head, vg, 0.0).astype(bf16)
            v_hi = jnp.where(low_head, 0.0, vg).astype(bf16)
            rhs = jnp.concatenate(
                [jnp.concatenate([v_lo[n * GMLP_BLOCK:(n + 1) * GMLP_BLOCK, :]
                                  for n in range(n_blocks)], axis=1),
                 jnp.concatenate([v_hi[n * GMLP_BLOCK:(n + 1) * GMLP_BLOCK, :]
                                  for n in range(n_blocks)], axis=1)],
                axis=0)
            mixed = jnp.dot(ws_pair_ref[g], rhs, preferred_element_type=f32)
            for n in range(n_blocks):
                yb_ref[n * GMLP_BLOCK:(n + 1) * GMLP_BLOCK, g * LANES:(g + 1) * LANES] = (
                    mixed[:, n * LANES:(n + 1) * LANES])
        yield
        u = in_proj(3 * c, gw)
        bias = jnp.concatenate([bias_ref[...]] * n_blocks, axis=0)
        y_b = _rms(u * (yb_ref[...] + bias), gg_ref[...]).astype(bf16)
        yield

        y = jnp.concatenate([y_a, y_b], axis=-1)
        x1 = x + jnp.dot(y, w_out_ref[...], preferred_element_type=f32)
        x1_ref[...] = x1
        h2_ref[...] = (x1 * g2_ref[...]).astype(bf16)
        r2 = _inv_rms(x1)
        rr2_ref[...] = jnp.broadcast_to(r2 * r2, rr2_ref.shape)
        yield

    def stage_b():
        rr2 = jnp.concatenate([rr2_ref[...]] * (ff_chunk // LANES), axis=1)
        acc = None
        for k in range(N_FF_CHUNKS):
            cols = slice(k * ff_chunk, (k + 1) * ff_chunk)
            a = jnp.maximum(
                jnp.dot(h2_ref[...], w_up_ref[:, cols], preferred_element_type=f32), 0.0)
            a = (a * a * rr2).astype(bf16)
            yield
            if acc is None:
                acc = x1_ref[...]
            acc = acc + jnp.dot(a, w_down_ref[cols, :], preferred_element_type=f32)
            if k == N_FF_CHUNKS - 1:
                pre_ref[...] = acc
                sq = acc * acc
                ssq_ref[...] = sum(sq[:, j:j + LANES] for j in range(0, sq.shape[1], LANES))
            yield

    def stage_c():
        pre = pre_ref[...]
        ms = jnp.sum(ssq_ref[...], axis=-1, keepdims=True) * (1.0 / pre.shape[1])
        o_buf[slot] = pre * lax.rsqrt(ms + EPS) * gf_ref[...]

    def run(order, after_group=None):
        b_groups = [i for i, name in enumerate(order) if name == "b"]
        if "a" in order and b_groups:
            assert order.rindex("a") > b_groups[0::2][-1]
        if "c" in order and b_groups:
            assert order.index("c") < b_groups[-1]
        gens = {}
        for name in order:
            if name == "c":
                stage_c()
                continue
            if name not in gens:
                gens[name] = {"a": stage_a, "b": stage_b}[name]()
            next(gens[name])
            if after_group is not None:
                after_group()
        for gen in gens.values():
            assert next(gen, "done") == "done"

    def first_step():
        _load_weight_as_bf16(w_in_hbm, w_in_ref)
        _load_weight_as_bf16(w_out_hbm, w_out_ref)
        prepare()

        def mix_while_loading(up_stage, up_sem, down_stage, down_sem):
            streams = (_WeightStream(w_up_hbm, w_up_ref, up_stage, up_sem),
                       _WeightStream(w_down_hbm, w_down_ref, down_stage, down_sem))
            for stream in streams:
                stream.start()

            def drain_some():
                for stream in streams:
                    stream.advance(OVERLAPPED_CHUNKS_PER_GROUP)

            run("aaaaaaa", after_group=drain_some)
            for stream in streams:
                stream.finish()

        pl.run_scoped(mix_while_loading,
                      *_WeightStream.staging(w_up_hbm, OVERLAPPED_LOAD_SLOTS),
                      *_WeightStream.staging(w_down_hbm, OVERLAPPED_LOAD_SLOTS))

    def x_copy(t, buf):
        return pltpu.make_async_copy(x_hbm.at[pl.ds(t * tile, tile), :], x_buf.at[buf], x_sem.at[buf])

    def out_copy(t, buf):
        return pltpu.make_async_copy(o_buf.at[buf], out_hbm.at[pl.ds(t * tile, tile), :], o_sem.at[buf])

    assert n_tiles >= 4

    step, slot = 0, 0
    x_copy(0, 0).start()
    x_copy(1, 1).start()
    x_copy(0, 0).wait()
    first_step()

    def steady_pass(t, carry):
        nonlocal step, slot
        step, slot = t, t % 2
        pl.when(t + 1 < n_tiles)(lambda: x_copy(t + 1, 1 - slot).start())
        x_copy(t, slot).wait()
        pl.when(t >= 4)(lambda: out_copy(t - 4, slot).wait())
        run("bc" "a" "aba" "ba" "bb" "aa" "bb" "ab")
        pl.when(t >= 2)(lambda: out_copy(t - 2, slot).start())
        return carry

    lax.fori_loop(1, n_tiles, steady_pass, 0)

    for t, stages in ((n_tiles, "bc" "bbbbbbb"), (n_tiles + 1, "c")):
        step, slot = t, t % 2
        out_copy(t - 4, slot).wait()
        run(stages)
        out_copy(t - 2, slot).start()
    for t in (n_tiles - 2, n_tiles - 1):
        out_copy(t, t % 2).wait()


def _resident(a):
    del a
    return pl.BlockSpec(memory_space=pltpu.VMEM)


def kernel(x, norm1_g, w_in, conv_w, gmlp_ln_g, gmlp_ln_b, gmlp_ws, gmlp_bs, out_norm_conv_g,
           out_norm_gmlp_g, w_out, norm2_g, w_up, w_down, final_g):
    depth = w_in.shape[0]
    batch, seq, d_model = x.shape
    c = conv_w.shape[1]
    gw = gmlp_ln_g.shape[1]
    head_dim = gw // GMLP_HEADS
    assert depth == 1, "one fused block per call"
    assert seq % TILE == 0 and TILE % GMLP_BLOCK == 0
    assert head_dim * HEADS_PER_VREG == LANES
    assert w_up.shape[2] % N_FF_CHUNKS == 0
    tiles_per_batch = seq // TILE
    n_tiles = batch * tiles_per_batch
    bf16 = jnp.bfloat16
    row = lambda a: a.reshape(1, -1)
    bias = jnp.repeat(jnp.transpose(gmlp_bs[0]), head_dim, axis=1)

    in_hbm = pl.BlockSpec(memory_space=pl.ANY)
    operands_and_specs = (
        (x.reshape(batch * seq, d_model), in_hbm),
        (row(norm1_g[0]), None), (w_in[0], in_hbm), (jnp.transpose(conv_w[0]), None),
        (row(gmlp_ln_g[0]), None), (row(gmlp_ln_b[0]), None), (gmlp_ws[0], None), (bias, None),
        (row(out_norm_conv_g[0]), None), (row(out_norm_gmlp_g[0]), None), (w_out[0], in_hbm),
        (row(norm2_g[0]), None), (w_up[0], in_hbm), (w_down[0], in_hbm), (row(final_g), None),
    )
    operands = [a for a, _ in operands_and_specs]
    in_specs = [_resident(a) if spec is None else spec for a, spec in operands_and_specs]
    out = pl.pallas_call(
        functools.partial(_block_kernel, n_tiles=n_tiles, tiles_per_batch=tiles_per_batch),
        in_specs=in_specs,
        out_specs=pl.BlockSpec(memory_space=pl.ANY),
        out_shape=jax.ShapeDtypeStruct((batch * seq, d_model), x.dtype),
        scratch_shapes=[
            pltpu.VMEM(w_in.shape[1:], bf16),
            pltpu.VMEM(w_out.shape[1:], bf16),
            pltpu.VMEM(w_up.shape[1:], bf16),
            pltpu.VMEM(w_down.shape[1:], bf16),
            pltpu.VMEM((GMLP_HEADS // HEADS_PER_VREG, GMLP_BLOCK, HEADS_PER_VREG * GMLP_BLOCK), bf16),
            pltpu.VMEM((TILE + SUBLANES, c), jnp.float32),
            pltpu.VMEM((TILE, gw), jnp.float32),
            pltpu.VMEM((TILE, d_model), jnp.float32),
            pltpu.VMEM((TILE, d_model), bf16),
            pltpu.VMEM((TILE, LANES), jnp.float32),
            pltpu.VMEM((TILE, d_model), jnp.float32),
            pltpu.VMEM((TILE, LANES), jnp.float32),
            pltpu.VMEM((2, TILE, d_model), jnp.float32),
            pltpu.VMEM((2, TILE, d_model), jnp.float32),
            pltpu.SemaphoreType.DMA((2,)),
            pltpu.SemaphoreType.DMA((2,)),
        ],
        compiler_params=pltpu.CompilerParams(vmem_limit_bytes=VMEM_LIMIT_BYTES),
        name="hybrid_block",
    )(*operands)
    return out.reshape(batch, seq, d_model)
```

```python
import functools

import jax
import jax.numpy as jnp
from jax import lax
from jax.experimental import pallas as pl
from jax.experimental.pallas import tpu as pltpu

EPS = 1e-6
CHUNK = 64
GMLP_BLOCK = 128
GMLP_HEADS = 8
LANES = 128
SUBLANES = 8
HEADS_PER_VREG = 2

TILE = 512
N_FF_CHUNKS = 4
VMEM_LIMIT_BYTES = 56 * 1024 * 1024

WEIGHT_CHUNK_BYTES = 1024 * 1024
WEIGHT_LOAD_SLOTS = 8
OVERLAPPED_LOAD_SLOTS = 4
OVERLAPPED_CHUNKS_PER_GROUP = 2


def _inv_rms(x):
    return lax.rsqrt(jnp.mean(x * x, axis=-1, keepdims=True) + EPS)


def _rms(x, g):
    return x * _inv_rms(x) * g


class _WeightStream:
    def __init__(self, src_hbm, dst_ref, stage_ref, sem_ref):
        self.src_hbm, self.dst_ref, self.stage_ref, self.sem_ref = src_hbm, dst_ref, stage_ref, sem_ref
        self.slots, self.rows, _ = stage_ref.shape
        assert src_hbm.shape[0] % self.rows == 0
        self.n_chunks = src_hbm.shape[0] // self.rows
        self.done = 0

    @staticmethod
    def staging(src_hbm, slots):
        n_cols = src_hbm.shape[1]
        rows = 1 << ((WEIGHT_CHUNK_BYTES // (n_cols * 4)).bit_length() - 1)
        return (pltpu.VMEM((slots, rows, n_cols), jnp.float32), pltpu.SemaphoreType.DMA((slots,)))

    def _copy(self, i, slot):
        return pltpu.make_async_copy(self.src_hbm.at[pl.ds(i * self.rows, self.rows), :],
                                     self.stage_ref.at[slot], self.sem_ref.at[slot])

    def start(self):
        for i in range(min(self.slots - 1, self.n_chunks)):
            self._copy(i, i).start()

    def advance(self, count):
        first, last = self.done, min(self.done + count, self.n_chunks)

        def step(i, carry):
            ahead = i + self.slots - 1

            @pl.when(ahead < self.n_chunks)
            def _():
                self._copy(ahead, ahead % self.slots).start()

            slot = i % self.slots
            self._copy(i, slot).wait()
            self.dst_ref[pl.ds(pl.multiple_of(i * self.rows, self.rows), self.rows), :] = (
                self.stage_ref[slot].astype(jnp.bfloat16))
            return carry

        lax.fori_loop(first, last, step, 0)
        self.done = last

    def finish(self):
        self.advance(self.n_chunks - self.done)


def _load_weight_as_bf16(src_hbm, dst_ref):
    def body(stage_ref, sem_ref):
        stream = _WeightStream(src_hbm, dst_ref, stage_ref, sem_ref)
        stream.start()
        stream.finish()

    pl.run_scoped(body, *_WeightStream.staging(src_hbm, WEIGHT_LOAD_SLOTS))


def _block_kernel(x_hbm, g1_ref, w_in_hbm, conv_w_ref, ln_g_ref, ln_b_ref, ws_ref, bias_ref,
                  gc_ref, gg_ref, w_out_hbm, g2_ref, w_up_hbm, w_down_hbm, gf_ref,
                  out_hbm, w_in_ref, w_out_ref, w_up_ref, w_down_ref,
                  ws_pair_ref, zbuf_ref, yb_ref, x1_ref, h2_ref, rr2_ref, pre_ref, ssq_ref,
                  x_buf, o_buf, x_sem, o_sem, *, n_tiles, tiles_per_batch):
    f32, bf16 = jnp.float32, jnp.bfloat16
    tile = TILE
    c = conv_w_ref.shape[1]
    gw = ln_g_ref.shape[1]
    ff_chunk = w_up_ref.shape[1] // N_FF_CHUNKS
    n_blocks = tile // GMLP_BLOCK
    n_pairs = GMLP_HEADS // HEADS_PER_VREG
    step = slot = None

    def prepare():
        qi = lax.broadcasted_iota(jnp.int32, (GMLP_BLOCK, GMLP_BLOCK), 0) // CHUNK
        kj = lax.broadcasted_iota(jnp.int32, (GMLP_BLOCK, GMLP_BLOCK), 1) // CHUNK
        causal = kj <= qi
        for h in range(GMLP_HEADS):
            g, r = divmod(h, HEADS_PER_VREG)
            ws_pair_ref[g, :, r * GMLP_BLOCK:(r + 1) * GMLP_BLOCK] = (
                jnp.where(causal, ws_ref[h], 0.0).astype(bf16))
        zbuf_ref[0:SUBLANES, :] = jnp.zeros((SUBLANES, c), f32)
        pre_ref[...] = jnp.zeros(pre_ref.shape, f32)
        ssq_ref[...] = jnp.zeros(ssq_ref.shape, f32)

    def stage_a():
        x = x_buf[slot]
        h = _rms(x, g1_ref[...]).astype(bf16)
        yield

        def in_proj(lo, width):
            return jnp.dot(h, w_in_ref[:, lo:lo + width], preferred_element_type=f32)

        z = in_proj(c, c) * in_proj(2 * c, c)
        zbuf_ref[SUBLANES:SUBLANES + tile, :] = z
        cw = conv_w_ref[...]
        conv = (zbuf_ref[SUBLANES - 2:SUBLANES - 2 + tile, :] * cw[0:1, :]
                + zbuf_ref[SUBLANES - 1:SUBLANES - 1 + tile, :] * cw[1:2, :]
                + z * cw[2:3, :])
        next_starts_sequence = (step + 1) % tiles_per_batch == 0
        zbuf_ref[0:SUBLANES, :] = jnp.where(next_starts_sequence, 0.0,
                                            zbuf_ref[tile:tile + SUBLANES, :])
        yield
        y_a = _rms(in_proj(0, c) * conv, gc_ref[...]).astype(bf16)
        yield

        v = in_proj(3 * c + gw, gw)
        mu = jnp.mean(v, axis=-1, keepdims=True)
        vc = v - mu
        vn = vc * lax.rsqrt(jnp.mean(vc * vc, axis=-1, keepdims=True) + EPS)
        vn = vn * ln_g_ref[...] + ln_b_ref[...]
        yield
        lane = lax.broadcasted_iota(jnp.int32, (tile, LANES), 1)
        low_head = lane < (LANES // HEADS_PER_VREG)
        for g in range(n_pairs):
            vg = vn[:, g * LANES:(g + 1) * LANES]
            v_lo = jnp.where(low_head, vg, 0.0).astype(bf16)
            v_hi = jnp.where(low_head, 0.0, vg).astype(bf16)
            rhs = jnp.concatenate(
                [jnp.concatenate([v_lo[n * GMLP_BLOCK:(n + 1) * GMLP_BLOCK, :]
                                  for n in range(n_blocks)], axis=1),
                 jnp.concatenate([v_hi[n * GMLP_BLOCK:(n + 1) * GMLP_BLOCK, :]
                                  for n in range(n_blocks)], axis=1)],
                axis=0)
            mixed = jnp.dot(ws_pair_ref[g], rhs, preferred_element_type=f32)
            for n in range(n_blocks):
                yb_ref[n * GMLP_BLOCK:(n + 1) * GMLP_BLOCK, g * LANES:(g + 1) * LANES] = (
                    mixed[:, n * LANES:(n + 1) * LANES])
        yield
        u = in_proj(3 * c, gw)
        bias = jnp.concatenate([bias_ref[...]] * n_blocks, axis=0)
        y_b = _rms(u * (yb_ref[...] + bias), gg_ref[...]).astype(bf16)
        yield

        y = jnp.concatenate([y_a, y_b], axis=-1)
        x1 = x + jnp.dot(y, w_out_ref[...], preferred_element_type=f32)
        x1_ref[...] = x1
        h2_ref[...] = (x1 * g2_ref[...]).astype(bf16)
        r2 = _inv_rms(x1)
        rr2_ref[...] = jnp.broadcast_to(r2 * r2, rr2_ref.shape)
        yield

    def stage_b():
        rr2 = jnp.concatenate([rr2_ref[...]] * (ff_chunk // LANES), axis=1)
        acc = None
        for k in range(N_FF_CHUNKS):
            cols = slice(k * ff_chunk, (k + 1) * ff_chunk)
            a = jnp.maximum(
                jnp.dot(h2_ref[...], w_up_ref[:, cols], preferred_element_type=f32), 0.0)
            a = (a * a * rr2).astype(bf16)
            yield
            if acc is None:
                acc = x1_ref[...]
            acc = acc + jnp.dot(a, w_down_ref[cols, :], preferred_element_type=f32)
            if k == N_FF_CHUNKS - 1:
                pre_ref[...] = acc
                sq = acc * acc
                ssq_ref[...] = sum(sq[:, j:j + LANES] for j in range(0, sq.shape[1], LANES))
            yield

    def stage_c():
        pre = pre_ref[...]
        ms = jnp.sum(ssq_ref[...], axis=-1, keepdims=True) * (1.0 / pre.shape[1])
        o_buf[slot] = pre * lax.rsqrt(ms + EPS) * gf_ref[...]

    def run(order, after_group=None):
        b_groups = [i for i, name in enumerate(order) if name == "b"]
        if "a" in order and b_groups:
            assert order.rindex("a") > b_groups[0::2][-1]
        if "c" in order and b_groups:
            assert order.index("c") < b_groups[-1]
        gens = {}
        for name in order:
            if name == "c":
                stage_c()
                continue
            if name not in gens:
                gens[name] = {"a": stage_a, "b": stage_b}[name]()
            next(gens[name])
            if after_group is not None:
                after_group()
        for gen in gens.values():
            assert next(gen, "done") == "done"

    def first_step():
        _load_weight_as_bf16(w_in_hbm, w_in_ref)
        _load_weight_as_bf16(w_out_hbm, w_out_ref)
        prepare()

        def mix_while_loading(up_stage, up_sem, down_stage, down_sem):
            streams = (_WeightStream(w_up_hbm, w_up_ref, up_stage, up_sem),
                       _WeightStream(w_down_hbm, w_down_ref, down_stage, down_sem))
            for stream in streams:
                stream.start()

            def drain_some():
                for stream in streams:
                    stream.advance(OVERLAPPED_CHUNKS_PER_GROUP)

            run("aaaaaaa", after_group=drain_some)
            for stream in streams:
                stream.finish()

        pl.run_scoped(mix_while_loading,
                      *_WeightStream.staging(w_up_hbm, OVERLAPPED_LOAD_SLOTS),
                      *_WeightStream.staging(w_down_hbm, OVERLAPPED_LOAD_SLOTS))

    def x_copy(t, buf):
        return pltpu.make_async_copy(x_hbm.at[pl.ds(t * tile, tile), :], x_buf.at[buf], x_sem.at[buf])

    def out_copy(t, buf):
        return pltpu.make_async_copy(o_buf.at[buf], out_hbm.at[pl.ds(t * tile, tile), :], o_sem.at[buf])

    assert n_tiles >= 4

    step, slot = 0, 0
    x_copy(0, 0).start()
    x_copy(1, 1).start()
    x_copy(0, 0).wait()
    first_step()

    def steady_pass(t, carry):
        nonlocal step, slot
        step, slot = t, t % 2
        pl.when(t + 1 < n_tiles)(lambda: x_copy(t + 1, 1 - slot).start())
        x_copy(t, slot).wait()
        pl.when(t >= 4)(lambda: out_copy(t - 4, slot).wait())
        run("bc" "ab" "aab" "ab" "b" "aa" "bb" "ab")
        pl.when(t >= 2)(lambda: out_copy(t - 2, slot).start())
        return carry

    lax.fori_loop(1, n_tiles, steady_pass, 0)

    for t, stages in ((n_tiles, "bc" "bbbbbbb"), (n_tiles + 1, "c")):
        step, slot = t, t % 2
        out_copy(t - 4, slot).wait()
        run(stages)
        out_copy(t - 2, slot).start()
    for t in (n_tiles - 2, n_tiles - 1):
        out_copy(t, t % 2).wait()


def _resident(a):
    del a
    return pl.BlockSpec(memory_space=pltpu.VMEM)


def kernel(x, norm1_g, w_in, conv_w, gmlp_ln_g, gmlp_ln_b, gmlp_ws, gmlp_bs, out_norm_conv_g,
           out_norm_gmlp_g, w_out, norm2_g, w_up, w_down, final_g):
    depth = w_in.shape[0]
    batch, seq, d_model = x.shape
    c = conv_w.shape[1]
    gw = gmlp_ln_g.shape[1]
    head_dim = gw // GMLP_HEADS
    assert depth == 1, "one fused block per call"
    assert seq % TILE == 0 and TILE % GMLP_BLOCK == 0
    assert head_dim * HEADS_PER_VREG == LANES
    assert w_up.shape[2] % N_FF_CHUNKS == 0
    tiles_per_batch = seq // TILE
    n_tiles = batch * tiles_per_batch
    bf16 = jnp.bfloat16
    row = lambda a: a.reshape(1, -1)
    bias = jnp.repeat(jnp.transpose(gmlp_bs[0]), head_dim, axis=1)

    in_hbm = pl.BlockSpec(memory_space=pl.ANY)
    operands_and_specs = (
        (x.reshape(batch * seq, d_model), in_hbm),
        (row(norm1_g[0]), None), (w_in[0], in_hbm), (jnp.transpose(conv_w[0]), None),
        (row(gmlp_ln_g[0]), None), (row(gmlp_ln_b[0]), None), (gmlp_ws[0], None), (bias, None),
        (row(out_norm_conv_g[0]), None), (row(out_norm_gmlp_g[0]), None), (w_out[0], in_hbm),
        (row(norm2_g[0]), None), (w_up[0], in_hbm), (w_down[0], in_hbm), (row(final_g), None),
    )
    operands = [a for a, _ in operands_and_specs]
    in_specs = [_resident(a) if spec is None else spec for a, spec in operands_and_specs]
    out = pl.pallas_call(
        functools.partial(_block_kernel, n_tiles=n_tiles, tiles_per_batch=tiles_per_batch),
        in_specs=in_specs,
        out_specs=pl.BlockSpec(memory_space=pl.ANY),
        out_shape=jax.ShapeDtypeStruct((batch * seq, d_model), x.dtype),
        scratch_shapes=[
            pltpu.VMEM(w_in.shape[1:], bf16),
            pltpu.VMEM(w_out.shape[1:], bf16),
            pltpu.VMEM(w_up.shape[1:], bf16),
            pltpu.VMEM(w_down.shape[1:], bf16),
            pltpu.VMEM((GMLP_HEADS // HEADS_PER_VREG, GMLP_BLOCK, HEADS_PER_VREG * GMLP_BLOCK), bf16),
            pltpu.VMEM((TILE + SUBLANES, c), jnp.float32),
            pltpu.VMEM((TILE, gw), jnp.float32),
            pltpu.VMEM((TILE, d_model), jnp.float32),
            pltpu.VMEM((TILE, d_model), bf16),
            pltpu.VMEM((TILE, LANES), jnp.float32),
            pltpu.VMEM((TILE, d_model), jnp.float32),
            pltpu.VMEM((TILE, LANES), jnp.float32),
            pltpu.VMEM((2, TILE, d_model), jnp.float32),
            pltpu.VMEM((2, TILE, d_model), jnp.float32),
            pltpu.SemaphoreType.DMA((2,)),
            pltpu.SemaphoreType.DMA((2,)),
        ],
        compiler_params=pltpu.CompilerParams(vmem_limit_bytes=VMEM_LIMIT_BYTES),
        name="hybrid_block",
    )(*operands)
    return out.reshape(batch, seq, d_model)
```

```python
import functools

import jax
import jax.numpy as jnp
from jax import lax
from jax.experimental import pallas as pl
from jax.experimental.pallas import tpu as pltpu

EPS = 1e-6
CHUNK = 64
GMLP_BLOCK = 128
GMLP_HEADS = 8
LANES = 128
SUBLANES = 8
HEADS_PER_VREG = 2

TILE = 512
N_FF_CHUNKS = 4
VMEM_LIMIT_BYTES = 56 * 1024 * 1024

WEIGHT_CHUNK_BYTES = 1024 * 1024
WEIGHT_LOAD_SLOTS = 8
OVERLAPPED_LOAD_SLOTS = 4
OVERLAPPED_CHUNKS_PER_GROUP = 2


def _inv_rms(x):
    return lax.rsqrt(jnp.mean(x * x, axis=-1, keepdims=True) + EPS)


def _rms(x, g):
    return x * _inv_rms(x) * g


class _WeightStream:
    def __init__(self, src_hbm, dst_ref, stage_ref, sem_ref):
        self.src_hbm, self.dst_ref, self.stage_ref, self.sem_ref = src_hbm, dst_ref, stage_ref, sem_ref
        self.slots, self.rows, _ = stage_ref.shape
        assert src_hbm.shape[0] % self.rows == 0
        self.n_chunks = src_hbm.shape[0] // self.rows
        self.done = 0

    @staticmethod
    def staging(src_hbm, slots):
        n_cols = src_hbm.shape[1]
        rows = 1 << ((WEIGHT_CHUNK_BYTES // (n_cols * 4)).bit_length() - 1)
        return (pltpu.VMEM((slots, rows, n_cols), jnp.float32), pltpu.SemaphoreType.DMA((slots,)))

    def _copy(self, i, slot):
        return pltpu.make_async_copy(self.src_hbm.at[pl.ds(i * self.rows, self.rows), :],
                                     self.stage_ref.at[slot], self.sem_ref.at[slot])

    def start(self):
        for i in range(min(self.slots - 1, self.n_chunks)):
            self._copy(i, i).start()

    def advance(self, count):
        first, last = self.done, min(self.done + count, self.n_chunks)

        def step(i, carry):
            ahead = i + self.slots - 1

            @pl.when(ahead < self.n_chunks)
            def _():
                self._copy(ahead, ahead % self.slots).start()

            slot = i % self.slots
            self._copy(i, slot).wait()
            self.dst_ref[pl.ds(pl.multiple_of(i * self.rows, self.rows), self.rows), :] = (
                self.stage_ref[slot].astype(jnp.bfloat16))
            return carry

        lax.fori_loop(first, last, step, 0)
        self.done = last

    def finish(self):
        self.advance(self.n_chunks - self.done)


def _load_weight_as_bf16(src_hbm, dst_ref):
    def body(stage_ref, sem_ref):
        stream = _WeightStream(src_hbm, dst_ref, stage_ref, sem_ref)
        stream.start()
        stream.finish()

    pl.run_scoped(body, *_WeightStream.staging(src_hbm, WEIGHT_LOAD_SLOTS))


def _block_kernel(x_hbm, g1_ref, w_in_hbm, conv_w_ref, ln_g_ref, ln_b_ref, ws_ref, bias_ref,
                  gc_ref, gg_ref, w_out_hbm, g2_ref, w_up_hbm, w_down_hbm, gf_ref,
                  out_hbm, w_in_ref, w_out_ref, w_up_ref, w_down_ref,
                  ws_pair_ref, zbuf_ref, yb_ref, x1_ref, h2_ref, rr2_ref, pre_ref, ssq_ref,
                  x_buf, o_buf, x_sem, o_sem, *, n_tiles, tiles_per_batch):
    f32, bf16 = jnp.float32, jnp.bfloat16
    tile = TILE
    c = conv_w_ref.shape[1]
    gw = ln_g_ref.shape[1]
    ff_chunk = w_up_ref.shape[1] // N_FF_CHUNKS
    n_blocks = tile // GMLP_BLOCK
    n_pairs = GMLP_HEADS // HEADS_PER_VREG
    step = slot = None

    def prepare():
        qi = lax.broadcasted_iota(jnp.int32, (GMLP_BLOCK, GMLP_BLOCK), 0) // CHUNK
        kj = lax.broadcasted_iota(jnp.int32, (GMLP_BLOCK, GMLP_BLOCK), 1) // CHUNK
        causal = kj <= qi
        for h in range(GMLP_HEADS):
            g, r = divmod(h, HEADS_PER_VREG)
            ws_pair_ref[g, :, r * GMLP_BLOCK:(r + 1) * GMLP_BLOCK] = (
                jnp.where(causal, ws_ref[h], 0.0).astype(bf16))
        zbuf_ref[0:SUBLANES, :] = jnp.zeros((SUBLANES, c), f32)
        pre_ref[...] = jnp.zeros(pre_ref.shape, f32)
        ssq_ref[...] = jnp.zeros(ssq_ref.shape, f32)

    def stage_a():
        x = x_buf[slot]
        h = _rms(x, g1_ref[...]).astype(bf16)
        yield

        def in_proj(lo, width):
            return jnp.dot(h, w_in_ref[:, lo:lo + width], preferred_element_type=f32)

        z = in_proj(c, c) * in_proj(2 * c, c)
        zbuf_ref[SUBLANES:SUBLANES + tile, :] = z
        cw = conv_w_ref[...]
        conv = (zbuf_ref[SUBLANES - 2:SUBLANES - 2 + tile, :] * cw[0:1, :]
                + zbuf_ref[SUBLANES - 1:SUBLANES - 1 + tile, :] * cw[1:2, :]
                + z * cw[2:3, :])
        next_starts_sequence = (step + 1) % tiles_per_batch == 0
        zbuf_ref[0:SUBLANES, :] = jnp.where(next_starts_sequence, 0.0,
                                            zbuf_ref[tile:tile + SUBLANES, :])
        yield
        y_a = _rms(in_proj(0, c) * conv, gc_ref[...]).astype(bf16)
        yield

        v = in_proj(3 * c + gw, gw)
        mu = jnp.mean(v, axis=-1, keepdims=True)
        vc = v - mu
        vn = vc * lax.rsqrt(jnp.mean(vc * vc, axis=-1, keepdims=True) + EPS)
        vn = vn * ln_g_ref[...] + ln_b_ref[...]
        yield
        lane = lax.broadcasted_iota(jnp.int32, (tile, LANES), 1)
        low_head = lane < (LANES // HEADS_PER_VREG)
        for g in range(n_pairs):
            vg = vn[:, g * LANES:(g + 1) * LANES]
            v_lo = jnp.where(low_head, vg, 0.0).astype(bf16)
            v_hi = jnp.where(low_head, 0.0, vg).astype(bf16)
            rhs = jnp.concatenate(
                [jnp.concatenate([v_lo[n * GMLP_BLOCK:(n + 1) * GMLP_BLOCK, :]
                                  for n in range(n_blocks)], axis=1),
                 jnp.concatenate([v_hi[n * GMLP_BLOCK:(n + 1) * GMLP_BLOCK, :]
                                  for n in range(n_blocks)], axis=1)],
                axis=0)
            mixed = jnp.dot(ws_pair_ref[g], rhs, preferred_element_type=f32)
            for n in range(n_blocks):
                yb_ref[n * GMLP_BLOCK:(n + 1) * GMLP_BLOCK, g * LANES:(g + 1) * LANES] = (
                    mixed[:, n * LANES:(n + 1) * LANES])
        yield
        u = in_proj(3 * c, gw)
        bias = jnp.concatenate([bias_ref[...]] * n_blocks, axis=0)
        y_b = _rms(u * (yb_ref[...] + bias), gg_ref[...]).astype(bf16)
        yield

        y = jnp.concatenate([y_a, y_b], axis=-1)
        x1 = x + jnp.dot(y, w_out_ref[...], preferred_element_type=f32)
        x1_ref[...] = x1
        h2_ref[...] = (x1 * g2_ref[...]).astype(bf16)
        r2 = _inv_rms(x1)
        rr2_ref[...] = jnp.broadcast_to(r2 * r2, rr2_ref.shape)
        yield

    def stage_b():
        rr2 = jnp.concatenate([rr2_ref[...]] * (ff_chunk // LANES), axis=1)
        acc = None
        for k in range(N_FF_CHUNKS):
            cols = slice(k * ff_chunk, (k + 1) * ff_chunk)
            a = jnp.maximum(
                jnp.dot(h2_ref[...], w_up_ref[:, cols], preferred_element_type=f32), 0.0)
            a = (a * a * rr2).astype(bf16)
            yield
            if acc is None:
                acc = x1_ref[...]
            acc = acc + jnp.dot(a, w_down_ref[cols, :], preferred_element_type=f32)
            if k == N_FF_CHUNKS - 1:
                pre_ref[...] = acc
                sq = acc * acc
                ssq_ref[...] = sum(sq[:, j:j + LANES] for j in range(0, sq.shape[1], LANES))
            yield

    def stage_c():
        pre = pre_ref[...]
        ms = jnp.sum(ssq_ref[...], axis=-1, keepdims=True) * (1.0 / pre.shape[1])
        o_buf[slot] = pre * lax.rsqrt(ms + EPS) * gf_ref[...]

    def run(order, after_group=None):
        b_groups = [i for i, name in enumerate(order) if name == "b"]
        if "a" in order and b_groups:
            assert order.rindex("a") > b_groups[0::2][-1]
        if "c" in order and b_groups:
            assert order.index("c") < b_groups[-1]
        gens = {}
        for name in order:
            if name == "c":
                stage_c()
                continue
            if name not in gens:
                gens[name] = {"a": stage_a, "b": stage_b}[name]()
            next(gens[name])
            if after_group is not None:
                after_group()
        for gen in gens.values():
            assert next(gen, "done") == "done"

    def first_step():
        _load_weight_as_bf16(w_in_hbm, w_in_ref)
        _load_weight_as_bf16(w_out_hbm, w_out_ref)
        prepare()

        def mix_while_loading(up_stage, up_sem, down_stage, down_sem):
            streams = (_WeightStream(w_up_hbm, w_up_ref, up_stage, up_sem),
                       _WeightStream(w_down_hbm, w_down_ref, down_stage, down_sem))
            for stream in streams:
                stream.start()

            def drain_some():
                for stream in streams:
                    stream.advance(OVERLAPPED_CHUNKS_PER_GROUP)

            run("aaaaaaa", after_group=drain_some)
            for stream in streams:
                stream.finish()

        pl.run_scoped(mix_while_loading,
                      *_WeightStream.staging(w_up_hbm, OVERLAPPED_LOAD_SLOTS),
                      *_WeightStream.staging(w_down_hbm, OVERLAPPED_LOAD_SLOTS))

    def x_copy(t, buf):
        return pltpu.make_async_copy(x_hbm.at[pl.ds(t * tile, tile), :], x_buf.at[buf], x_sem.at[buf])

    def out_copy(t, buf):
        return pltpu.make_async_copy(o_buf.at[buf], out_hbm.at[pl.ds(t * tile, tile), :], o_sem.at[buf])

    assert n_tiles >= 4

    step, slot = 0, 0
    x_copy(0, 0).start()
    x_copy(1, 1).start()
    x_copy(0, 0).wait()
    first_step()

    def steady_pass(t, carry):
        nonlocal step, slot
        step, slot = t, t % 2
        pl.when(t + 1 < n_tiles)(lambda: x_copy(t + 1, 1 - slot).start())
        x_copy(t, slot).wait()
        pl.when(t >= 4)(lambda: out_copy(t - 4, slot).wait())
        run("bc" "ab" "aab" "ab" "ab" "ab" "b" "ab")
        pl.when(t >= 2)(lambda: out_copy(t - 2, slot).start())
        return carry

    lax.fori_loop(1, n_tiles, steady_pass, 0)

    for t, stages in ((n_tiles, "bc" "bbbbbbb"), (n_tiles + 1, "c")):
        step, slot = t, t % 2
        out_copy(t - 4, slot).wait()
        run(stages)
        out_copy(t - 2, slot).start()
    for t in (n_tiles - 2, n_tiles - 1):
        out_copy(t, t % 2).wait()


def _resident(a):
    del a
    return pl.BlockSpec(memory_space=pltpu.VMEM)


def kernel(x, norm1_g, w_in, conv_w, gmlp_ln_g, gmlp_ln_b, gmlp_ws, gmlp_bs, out_norm_conv_g,
           out_norm_gmlp_g, w_out, norm2_g, w_up, w_down, final_g):
    depth = w_in.shape[0]
    batch, seq, d_model = x.shape
    c = conv_w.shape[1]
    gw = gmlp_ln_g.shape[1]
    head_dim = gw // GMLP_HEADS
    assert depth == 1, "one fused block per call"
    assert seq % TILE == 0 and TILE % GMLP_BLOCK == 0
    assert head_dim * HEADS_PER_VREG == LANES
    assert w_up.shape[2] % N_FF_CHUNKS == 0
    tiles_per_batch = seq // TILE
    n_tiles = batch * tiles_per_batch
    bf16 = jnp.bfloat16
    row = lambda a: a.reshape(1, -1)
    bias = jnp.repeat(jnp.transpose(gmlp_bs[0]), head_dim, axis=1)

    in_hbm = pl.BlockSpec(memory_space=pl.ANY)
    operands_and_specs = (
        (x.reshape(batch * seq, d_model), in_hbm),
        (row(norm1_g[0]), None), (w_in[0], in_hbm), (jnp.transpose(conv_w[0]), None),
        (row(gmlp_ln_g[0]), None), (row(gmlp_ln_b[0]), None), (gmlp_ws[0], None), (bias, None),
        (row(out_norm_conv_g[0]), None), (row(out_norm_gmlp_g[0]), None), (w_out[0], in_hbm),
        (row(norm2_g[0]), None), (w_up[0], in_hbm), (w_down[0], in_hbm), (row(final_g), None),
    )
    operands = [a for a, _ in operands_and_specs]
    in_specs = [_resident(a) if spec is None else spec for a, spec in operands_and_specs]
    out = pl.pallas_call(
        functools.partial(_block_kernel, n_tiles=n_tiles, tiles_per_batch=tiles_per_batch),
        in_specs=in_specs,
        out_specs=pl.BlockSpec(memory_space=pl.ANY),
        out_shape=jax.ShapeDtypeStruct((batch * seq, d_model), x.dtype),
        scratch_shapes=[
            pltpu.VMEM(w_in.shape[1:], bf16),
            pltpu.VMEM(w_out.shape[1:], bf16),
            pltpu.VMEM(w_up.shape[1:], bf16),
            pltpu.VMEM(w_down.shape[1:], bf16),
            pltpu.VMEM((GMLP_HEADS // HEADS_PER_VREG, GMLP_BLOCK, HEADS_PER_VREG * GMLP_BLOCK), bf16),
            pltpu.VMEM((TILE + SUBLANES, c), jnp.float32),
            pltpu.VMEM((TILE, gw), jnp.float32),
            pltpu.VMEM((TILE, d_model), jnp.float32),
            pltpu.VMEM((TILE, d_model), bf16),
            pltpu.VMEM((TILE, LANES), jnp.float32),
            pltpu.VMEM((TILE, d_model), jnp.float32),
            pltpu.VMEM((TILE, LANES), jnp.float32),
            pltpu.VMEM((2, TILE, d_model), jnp.float32),
            pltpu.VMEM((2, TILE, d_model), jnp.float32),
            pltpu.SemaphoreType.DMA((2,)),
            pltpu.SemaphoreType.DMA((2,)),
        ],
        compiler_params=pltpu.CompilerParams(vmem_limit_bytes=VMEM_LIMIT_BYTES),
        name="hybrid_block",
    )(*operands)
    return out.reshape(batch, seq, d_model)
```

```python
import functools

import jax
import jax.numpy as jnp
from jax import lax
from jax.experimental import pallas as pl
from jax.experimental.pallas import tpu as pltpu

EPS = 1e-6
CHUNK = 64
GMLP_BLOCK = 128
GMLP_HEADS = 8
LANES = 128
SUBLANES = 8
HEADS_PER_VREG = 2

TILE = 512
N_FF_CHUNKS = 4
VMEM_LIMIT_BYTES = 60000 * 1024

WEIGHT_CHUNK_BYTES = 1024 * 1024
WEIGHT_LOAD_SLOTS = 8
OVERLAPPED_LOAD_SLOTS = 5
OVERLAPPED_CHUNKS_PER_GROUP = 2


def _inv_rms(x):
    return lax.rsqrt(jnp.mean(x * x, axis=-1, keepdims=True) + EPS)


def _rms(x, g):
    return x * _inv_rms(x) * g


class _WeightStream:
    def __init__(self, src_hbm, dst_ref, stage_ref, sem_ref):
        self.src_hbm, self.dst_ref, self.stage_ref, self.sem_ref = src_hbm, dst_ref, stage_ref, sem_ref
        self.slots, self.rows, _ = stage_ref.shape
        assert src_hbm.shape[0] % self.rows == 0
        self.n_chunks = src_hbm.shape[0] // self.rows
        self.done = 0

    @staticmethod
    def staging(src_hbm, slots):
        n_cols = src_hbm.shape[1]
        rows = 1 << ((WEIGHT_CHUNK_BYTES // (n_cols * 4)).bit_length() - 1)
        return (pltpu.VMEM((slots, rows, n_cols), jnp.float32), pltpu.SemaphoreType.DMA((slots,)))

    def _copy(self, i, slot):
        return pltpu.make_async_copy(self.src_hbm.at[pl.ds(i * self.rows, self.rows), :],
                                     self.stage_ref.at[slot], self.sem_ref.at[slot])

    def start(self):
        for i in range(min(self.slots - 1, self.n_chunks)):
            self._copy(i, i).start()

    def advance(self, count):
        first, last = self.done, min(self.done + count, self.n_chunks)

        def step(i, carry):
            ahead = i + self.slots - 1

            @pl.when(ahead < self.n_chunks)
            def _():
                self._copy(ahead, ahead % self.slots).start()

            slot = i % self.slots
            self._copy(i, slot).wait()
            self.dst_ref[pl.ds(pl.multiple_of(i * self.rows, self.rows), self.rows), :] = (
                self.stage_ref[slot].astype(jnp.bfloat16))
            return carry

        lax.fori_loop(first, last, step, 0)
        self.done = last

    def finish(self):
        self.advance(self.n_chunks - self.done)


def _load_weight_as_bf16(src_hbm, dst_ref):
    def body(stage_ref, sem_ref):
        stream = _WeightStream(src_hbm, dst_ref, stage_ref, sem_ref)
        stream.start()
        stream.finish()

    pl.run_scoped(body, *_WeightStream.staging(src_hbm, WEIGHT_LOAD_SLOTS))


def _block_kernel(x_hbm, g1_ref, w_in_hbm, conv_w_ref, ln_g_ref, ln_b_ref, ws_ref, bias_ref,
                  gc_ref, gg_ref, w_out_hbm, g2_ref, w_up_hbm, w_down_hbm, gf_ref,
                  out_hbm, w_in_ref, w_out_ref, w_up_ref, w_down_ref,
                  ws_pair_ref, zbuf_ref, yb_ref, x1_ref, h2_ref, rr2_ref, pre_ref, ssq_ref,
                  x_buf, o_buf, x_sem, o_sem, *, n_tiles, tiles_per_batch):
    f32, bf16 = jnp.float32, jnp.bfloat16
    tile = TILE
    c = conv_w_ref.shape[1]
    gw = ln_g_ref.shape[1]
    ff_chunk = w_up_ref.shape[1] // N_FF_CHUNKS
    n_blocks = tile // GMLP_BLOCK
    n_pairs = GMLP_HEADS // HEADS_PER_VREG
    step = slot = None

    def prepare():
        qi = lax.broadcasted_iota(jnp.int32, (GMLP_BLOCK, GMLP_BLOCK), 0) // CHUNK
        kj = lax.broadcasted_iota(jnp.int32, (GMLP_BLOCK, GMLP_BLOCK), 1) // CHUNK
        causal = kj <= qi
        for h in range(GMLP_HEADS):
            g, r = divmod(h, HEADS_PER_VREG)
            ws_pair_ref[g, :, r * GMLP_BLOCK:(r + 1) * GMLP_BLOCK] = (
                jnp.where(causal, ws_ref[h], 0.0).astype(bf16))
        zbuf_ref[0:SUBLANES, :] = jnp.zeros((SUBLANES, c), f32)
        pre_ref[...] = jnp.zeros(pre_ref.shape, f32)
        ssq_ref[...] = jnp.zeros(ssq_ref.shape, f32)

    def stage_a():
        x = x_buf[slot]
        h = _rms(x, g1_ref[...]).astype(bf16)
        yield

        def in_proj(lo, width):
            return jnp.dot(h, w_in_ref[:, lo:lo + width], preferred_element_type=f32)

        z = in_proj(c, c) * in_proj(2 * c, c)
        zbuf_ref[SUBLANES:SUBLANES + tile, :] = z
        cw = conv_w_ref[...]
        conv = (zbuf_ref[SUBLANES - 2:SUBLANES - 2 + tile, :] * cw[0:1, :]
                + zbuf_ref[SUBLANES - 1:SUBLANES - 1 + tile, :] * cw[1:2, :]
                + z * cw[2:3, :])
        next_starts_sequence = (step + 1) % tiles_per_batch == 0
        zbuf_ref[0:SUBLANES, :] = jnp.where(next_starts_sequence, 0.0,
                                            zbuf_ref[tile:tile + SUBLANES, :])
        yield
        y_a = _rms(in_proj(0, c) * conv, gc_ref[...]).astype(bf16)
        yield

        v = in_proj(3 * c + gw, gw)
        mu = jnp.mean(v, axis=-1, keepdims=True)
        vc = v - mu
        vn = vc * lax.rsqrt(jnp.mean(vc * vc, axis=-1, keepdims=True) + EPS)
        vn = vn * ln_g_ref[...] + ln_b_ref[...]
        yield
        lane = lax.broadcasted_iota(jnp.int32, (tile, LANES), 1)
        low_head = lane < (LANES // HEADS_PER_VREG)
        for g in range(n_pairs):
            vg = vn[:, g * LANES:(g + 1) * LANES]
            v_lo = jnp.where(low_head, vg, 0.0).astype(bf16)
            v_hi = jnp.where(low_head, 0.0, vg).astype(bf16)
            rhs = jnp.concatenate(
                [jnp.concatenate([v_lo[n * GMLP_BLOCK:(n + 1) * GMLP_BLOCK, :]
                                  for n in range(n_blocks)], axis=1),
                 jnp.concatenate([v_hi[n * GMLP_BLOCK:(n + 1) * GMLP_BLOCK, :]
                                  for n in range(n_blocks)], axis=1)],
                axis=0)
            mixed = jnp.dot(ws_pair_ref[g], rhs, preferred_element_type=f32)
            for n in range(n_blocks):
                yb_ref[n * GMLP_BLOCK:(n + 1) * GMLP_BLOCK, g * LANES:(g + 1) * LANES] = (
                    mixed[:, n * LANES:(n + 1) * LANES])
        yield
        u = in_proj(3 * c, gw)
        bias = jnp.concatenate([bias_ref[...]] * n_blocks, axis=0)
        y_b = _rms(u * (yb_ref[...] + bias), gg_ref[...]).astype(bf16)
        yield

        y = jnp.concatenate([y_a, y_b], axis=-1)
        x1 = x + jnp.dot(y, w_out_ref[...], preferred_element_type=f32)
        x1_ref[...] = x1
        h2_ref[...] = (x1 * g2_ref[...]).astype(bf16)
        r2 = _inv_rms(x1)
        rr2_ref[...] = jnp.broadcast_to(r2 * r2, rr2_ref.shape)
        yield

    def stage_b():
        rr2 = jnp.concatenate([rr2_ref[...]] * (ff_chunk // LANES), axis=1)
        acc = None
        for k in range(N_FF_CHUNKS):
            cols = slice(k * ff_chunk, (k + 1) * ff_chunk)
            a = jnp.maximum(
                jnp.dot(h2_ref[...], w_up_ref[:, cols], preferred_element_type=f32), 0.0)
            a = (a * a * rr2).astype(bf16)
            yield
            if acc is None:
                acc = x1_ref[...]
            acc = acc + jnp.dot(a, w_down_ref[cols, :], preferred_element_type=f32)
            if k == N_FF_CHUNKS - 1:
                pre_ref[...] = acc
                sq = acc * acc
                ssq_ref[...] = sum(sq[:, j:j + LANES] for j in range(0, sq.shape[1], LANES))
            yield

    def stage_c():
        pre = pre_ref[...]
        ms = jnp.sum(ssq_ref[...], axis=-1, keepdims=True) * (1.0 / pre.shape[1])
        o_buf[slot] = pre * lax.rsqrt(ms + EPS) * gf_ref[...]

    def run(order, after_group=None):
        b_groups = [i for i, name in enumerate(order) if name == "b"]
        if "a" in order and b_groups:
            assert order.rindex("a") > b_groups[0::2][-1]
        if "c" in order and b_groups:
            assert order.index("c") < b_groups[-1]
        gens = {}
        for name in order:
            if name == "c":
                stage_c()
                continue
            if name not in gens:
                gens[name] = {"a": stage_a, "b": stage_b}[name]()
            next(gens[name])
            if after_group is not None:
                after_group()
        for gen in gens.values():
            assert next(gen, "done") == "done"

    def first_step():
        _load_weight_as_bf16(w_in_hbm, w_in_ref)
        _load_weight_as_bf16(w_out_hbm, w_out_ref)
        prepare()

        def mix_while_loading(up_stage, up_sem, down_stage, down_sem):
            streams = (_WeightStream(w_up_hbm, w_up_ref, up_stage, up_sem),
                       _WeightStream(w_down_hbm, w_down_ref, down_stage, down_sem))
            for stream in streams:
                stream.start()

            def drain_some():
                for stream in streams:
                    stream.advance(OVERLAPPED_CHUNKS_PER_GROUP)

            run("aaaaaaa", after_group=drain_some)
            for stream in streams:
                stream.finish()

        pl.run_scoped(mix_while_loading,
                      *_WeightStream.staging(w_up_hbm, OVERLAPPED_LOAD_SLOTS),
                      *_WeightStream.staging(w_down_hbm, OVERLAPPED_LOAD_SLOTS))

    def x_copy(t, buf):
        return pltpu.make_async_copy(x_hbm.at[pl.ds(t * tile, tile), :], x_buf.at[buf], x_sem.at[buf])

    def out_copy(t, buf):
        return pltpu.make_async_copy(o_buf.at[buf], out_hbm.at[pl.ds(t * tile, tile), :], o_sem.at[buf])

    assert n_tiles >= 4

    step, slot = 0, 0
    x_copy(0, 0).start()
    x_copy(1, 1).start()
    x_copy(0, 0).wait()
    first_step()

    def steady_pass(t, carry):
        nonlocal step, slot
        step, slot = t, t % 2
        pl.when(t + 1 < n_tiles)(lambda: x_copy(t + 1, 1 - slot).start())
        x_copy(t, slot).wait()
        pl.when(t >= 4)(lambda: out_copy(t - 4, slot).wait())
        run("bc" "ab" "aab" "ab" "b" "aa" "bb" "ab")
        pl.when(t >= 2)(lambda: out_copy(t - 2, slot).start())
        return carry

    lax.fori_loop(1, n_tiles, steady_pass, 0)

    for t, stages in ((n_tiles, "bc" "bbbbbbb"), (n_tiles + 1, "c")):
        step, slot = t, t % 2
        out_copy(t - 4, slot).wait()
        run(stages)
        out_copy(t - 2, slot).start()
    for t in (n_tiles - 2, n_tiles - 1):
        out_copy(t, t % 2).wait()


def _resident(a):
    del a
    return pl.BlockSpec(memory_space=pltpu.VMEM)


def kernel(x, norm1_g, w_in, conv_w, gmlp_ln_g, gmlp_ln_b, gmlp_ws, gmlp_bs, out_norm_conv_g,
           out_norm_gmlp_g, w_out, norm2_g, w_up, w_down, final_g):
    depth = w_in.shape[0]
    batch, seq, d_model = x.shape
    c = conv_w.shape[1]
    gw = gmlp_ln_g.shape[1]
    head_dim = gw // GMLP_HEADS
    assert depth == 1, "one fused block per call"
    assert seq % TILE == 0 and TILE % GMLP_BLOCK == 0
    assert head_dim * HEADS_PER_VREG == LANES
    assert w_up.shape[2] % N_FF_CHUNKS == 0
    tiles_per_batch = seq // TILE
    n_tiles = batch * tiles_per_batch
    bf16 = jnp.bfloat16
    row = lambda a: a.reshape(1, -1)
    bias = jnp.repeat(jnp.transpose(gmlp_bs[0]), head_dim, axis=1)

    in_hbm = pl.BlockSpec(memory_space=pl.ANY)
    operands_and_specs = (
        (x.reshape(batch * seq, d_model), in_hbm),
        (row(norm1_g[0]), None), (w_in[0], in_hbm), (jnp.transpose(conv_w[0]), None),
        (row(gmlp_ln_g[0]), None), (row(gmlp_ln_b[0]), None), (gmlp_ws[0], None), (bias, None),
        (row(out_norm_conv_g[0]), None), (row(out_norm_gmlp_g[0]), None), (w_out[0], in_hbm),
        (row(norm2_g[0]), None), (w_up[0], in_hbm), (w_down[0], in_hbm), (row(final_g), None),
    )
    operands = [a for a, _ in operands_and_specs]
    in_specs = [_resident(a) if spec is None else spec for a, spec in operands_and_specs]
    out = pl.pallas_call(
        functools.partial(_block_kernel, n_tiles=n_tiles, tiles_per_batch=tiles_per_batch),
        in_specs=in_specs,
        out_specs=pl.BlockSpec(memory_space=pl.ANY),
        out_shape=jax.ShapeDtypeStruct((batch * seq, d_model), x.dtype),
        scratch_shapes=[
            pltpu.VMEM(w_in.shape[1:], bf16),
            pltpu.VMEM(w_out.shape[1:], bf16),
            pltpu.VMEM(w_up.shape[1:], bf16),
            pltpu.VMEM(w_down.shape[1:], bf16),
            pltpu.VMEM((GMLP_HEADS // HEADS_PER_VREG, GMLP_BLOCK, HEADS_PER_VREG * GMLP_BLOCK), bf16),
            pltpu.VMEM((TILE + SUBLANES, c), jnp.float32),
            pltpu.VMEM((TILE, gw), jnp.float32),
            pltpu.VMEM((TILE, d_model), jnp.float32),
            pltpu.VMEM((TILE, d_model), bf16),
            pltpu.VMEM((TILE, LANES), jnp.float32),
            pltpu.VMEM((TILE, d_model), jnp.float32),
            pltpu.VMEM((TILE, LANES), jnp.float32),
            pltpu.VMEM((2, TILE, d_model), jnp.float32),
            pltpu.VMEM((2, TILE, d_model), jnp.float32),
            pltpu.SemaphoreType.DMA((2,)),
            pltpu.SemaphoreType.DMA((2,)),
        ],
        compiler_params=pltpu.CompilerParams(vmem_limit_bytes=VMEM_LIMIT_BYTES),
        name="hybrid_block",
    )(*operands)
    return out.reshape(batch, seq, d_model)
```

```python
import functools

import jax
import jax.numpy as jnp
from jax import lax
from jax.experimental import pallas as pl
from jax.experimental.pallas import tpu as pltpu

EPS = 1e-6
CHUNK = 64
GMLP_BLOCK = 128
GMLP_HEADS = 8
LANES = 128
SUBLANES = 8
HEADS_PER_VREG = 2

TILE = 512
N_FF_CHUNKS = 4
VMEM_LIMIT_BYTES = 60000 * 1024

WEIGHT_CHUNK_BYTES = 1024 * 1024
WEIGHT_LOAD_SLOTS = 8
OVERLAPPED_CHUNK_BYTES = 512 * 1024
OVERLAPPED_LOAD_SLOTS = 10
OVERLAPPED_CHUNKS_PER_GROUP = 4


def _inv_rms(x):
    return lax.rsqrt(jnp.mean(x * x, axis=-1, keepdims=True) + EPS)


def _rms(x, g):
    return x * _inv_rms(x) * g


class _WeightStream:
    def __init__(self, src_hbm, dst_ref, stage_ref, sem_ref):
        self.src_hbm, self.dst_ref, self.stage_ref, self.sem_ref = src_hbm, dst_ref, stage_ref, sem_ref
        self.slots, self.rows, _ = stage_ref.shape
        assert src_hbm.shape[0] % self.rows == 0
        self.n_chunks = src_hbm.shape[0] // self.rows
        self.done = 0

    @staticmethod
    def staging(src_hbm, slots, chunk_bytes=WEIGHT_CHUNK_BYTES):
        n_cols = src_hbm.shape[1]
        rows = 1 << ((chunk_bytes // (n_cols * 4)).bit_length() - 1)
        return (pltpu.VMEM((slots, rows, n_cols), jnp.float32), pltpu.SemaphoreType.DMA((slots,)))

    def _copy(self, i, slot):
        return pltpu.make_async_copy(self.src_hbm.at[pl.ds(i * self.rows, self.rows), :],
                                     self.stage_ref.at[slot], self.sem_ref.at[slot])

    def start(self):
        for i in range(min(self.slots - 1, self.n_chunks)):
            self._copy(i, i).start()

    def advance(self, count):
        first, last = self.done, min(self.done + count, self.n_chunks)

        def step(i, carry):
            ahead = i + self.slots - 1

            @pl.when(ahead < self.n_chunks)
            def _():
                self._copy(ahead, ahead % self.slots).start()

            slot = i % self.slots
            self._copy(i, slot).wait()
            self.dst_ref[pl.ds(pl.multiple_of(i * self.rows, self.rows), self.rows), :] = (
                self.stage_ref[slot].astype(jnp.bfloat16))
            return carry

        lax.fori_loop(first, last, step, 0)
        self.done = last

    def finish(self):
        self.advance(self.n_chunks - self.done)


def _load_weight_as_bf16(src_hbm, dst_ref):
    def body(stage_ref, sem_ref):
        stream = _WeightStream(src_hbm, dst_ref, stage_ref, sem_ref)
        stream.start()
        stream.finish()

    pl.run_scoped(body, *_WeightStream.staging(src_hbm, WEIGHT_LOAD_SLOTS))


def _block_kernel(x_hbm, g1_ref, w_in_hbm, conv_w_ref, ln_g_ref, ln_b_ref, ws_ref, bias_ref,
                  gc_ref, gg_ref, w_out_hbm, g2_ref, w_up_hbm, w_down_hbm, gf_ref,
                  out_hbm, w_in_ref, w_out_ref, w_up_ref, w_down_ref,
                  ws_pair_ref, zbuf_ref, yb_ref, x1_ref, h2_ref, rr2_ref, pre_ref, ssq_ref,
                  x_buf, o_buf, x_sem, o_sem, *, n_tiles, tiles_per_batch):
    f32, bf16 = jnp.float32, jnp.bfloat16
    tile = TILE
    c = conv_w_ref.shape[1]
    gw = ln_g_ref.shape[1]
    ff_chunk = w_up_ref.shape[1] // N_FF_CHUNKS
    n_blocks = tile // GMLP_BLOCK
    n_pairs = GMLP_HEADS // HEADS_PER_VREG
    step = slot = None

    def prepare():
        qi = lax.broadcasted_iota(jnp.int32, (GMLP_BLOCK, GMLP_BLOCK), 0) // CHUNK
        kj = lax.broadcasted_iota(jnp.int32, (GMLP_BLOCK, GMLP_BLOCK), 1) // CHUNK
        causal = kj <= qi
        for h in range(GMLP_HEADS):
            g, r = divmod(h, HEADS_PER_VREG)
            ws_pair_ref[g, :, r * GMLP_BLOCK:(r + 1) * GMLP_BLOCK] = (
                jnp.where(causal, ws_ref[h], 0.0).astype(bf16))
        zbuf_ref[0:SUBLANES, :] = jnp.zeros((SUBLANES, c), f32)
        pre_ref[...] = jnp.zeros(pre_ref.shape, f32)
        ssq_ref[...] = jnp.zeros(ssq_ref.shape, f32)

    def stage_a():
        x = x_buf[slot]
        h = _rms(x, g1_ref[...]).astype(bf16)
        yield

        def in_proj(lo, width):
            return jnp.dot(h, w_in_ref[:, lo:lo + width], preferred_element_type=f32)

        z = in_proj(c, c) * in_proj(2 * c, c)
        zbuf_ref[SUBLANES:SUBLANES + tile, :] = z
        cw = conv_w_ref[...]
        conv = (zbuf_ref[SUBLANES - 2:SUBLANES - 2 + tile, :] * cw[0:1, :]
                + zbuf_ref[SUBLANES - 1:SUBLANES - 1 + tile, :] * cw[1:2, :]
                + z * cw[2:3, :])
        next_starts_sequence = (step + 1) % tiles_per_batch == 0
        zbuf_ref[0:SUBLANES, :] = jnp.where(next_starts_sequence, 0.0,
                                            zbuf_ref[tile:tile + SUBLANES, :])
        yield
        y_a = _rms(in_proj(0, c) * conv, gc_ref[...]).astype(bf16)
        yield

        v = in_proj(3 * c + gw, gw)
        mu = jnp.mean(v, axis=-1, keepdims=True)
        vc = v - mu
        vn = vc * lax.rsqrt(jnp.mean(vc * vc, axis=-1, keepdims=True) + EPS)
        vn = vn * ln_g_ref[...] + ln_b_ref[...]
        yield
        lane = lax.broadcasted_iota(jnp.int32, (tile, LANES), 1)
        low_head = lane < (LANES // HEADS_PER_VREG)
        for g in range(n_pairs):
            vg = vn[:, g * LANES:(g + 1) * LANES]
            v_lo = jnp.where(low_head, vg, 0.0).astype(bf16)
            v_hi = jnp.where(low_head, 0.0, vg).astype(bf16)
            rhs = jnp.concatenate(
                [jnp.concatenate([v_lo[n * GMLP_BLOCK:(n + 1) * GMLP_BLOCK, :]
                                  for n in range(n_blocks)], axis=1),
                 jnp.concatenate([v_hi[n * GMLP_BLOCK:(n + 1) * GMLP_BLOCK, :]
                                  for n in range(n_blocks)], axis=1)],
                axis=0)
            mixed = jnp.dot(ws_pair_ref[g], rhs, preferred_element_type=f32)
            for n in range(n_blocks):
                yb_ref[n * GMLP_BLOCK:(n + 1) * GMLP_BLOCK, g * LANES:(g + 1) * LANES] = (
                    mixed[:, n * LANES:(n + 1) * LANES])
        yield
        u = in_proj(3 * c, gw)
        bias = jnp.concatenate([bias_ref[...]] * n_blocks, axis=0)
        y_b = _rms(u * (yb_ref[...] + bias), gg_ref[...]).astype(bf16)
        yield

        y = jnp.concatenate([y_a, y_b], axis=-1)
        x1 = x + jnp.dot(y, w_out_ref[...], preferred_element_type=f32)
        x1_ref[...] = x1
        h2_ref[...] = (x1 * g2_ref[...]).astype(bf16)
        r2 = _inv_rms(x1)
        rr2_ref[...] = jnp.broadcast_to(r2 * r2, rr2_ref.shape)
        yield

    def stage_b():
        rr2 = jnp.concatenate([rr2_ref[...]] * (ff_chunk // LANES), axis=1)
        acc = None
        for k in range(N_FF_CHUNKS):
            cols = slice(k * ff_chunk, (k + 1) * ff_chunk)
            a = jnp.maximum(
                jnp.dot(h2_ref[...], w_up_ref[:, cols], preferred_element_type=f32), 0.0)
            a = (a * a * rr2).astype(bf16)
            yield
            if acc is None:
                acc = x1_ref[...]
            acc = acc + jnp.dot(a, w_down_ref[cols, :], preferred_element_type=f32)
            if k == N_FF_CHUNKS - 1:
                pre_ref[...] = acc
                sq = acc * acc
                ssq_ref[...] = sum(sq[:, j:j + LANES] for j in range(0, sq.shape[1], LANES))
            yield

    def stage_c():
        pre = pre_ref[...]
        ms = jnp.sum(ssq_ref[...], axis=-1, keepdims=True) * (1.0 / pre.shape[1])
        o_buf[slot] = pre * lax.rsqrt(ms + EPS) * gf_ref[...]

    def run(order, after_group=None):
        b_groups = [i for i, name in enumerate(order) if name == "b"]
        if "a" in order and b_groups:
            assert order.rindex("a") > b_groups[0::2][-1]
        if "c" in order and b_groups:
            assert order.index("c") < b_groups[-1]
        gens = {}
        for name in order:
            if name == "c":
                stage_c()
                continue
            if name not in gens:
                gens[name] = {"a": stage_a, "b": stage_b}[name]()
            next(gens[name])
            if after_group is not None:
                after_group()
        for gen in gens.values():
            assert next(gen, "done") == "done"

    def first_step():
        _load_weight_as_bf16(w_in_hbm, w_in_ref)
        _load_weight_as_bf16(w_out_hbm, w_out_ref)
        prepare()

        def mix_while_loading(up_stage, up_sem, down_stage, down_sem):
            streams = (_WeightStream(w_up_hbm, w_up_ref, up_stage, up_sem),
                       _WeightStream(w_down_hbm, w_down_ref, down_stage, down_sem))
            for stream in streams:
                stream.start()

            def drain_some():
                for stream in streams:
                    stream.advance(OVERLAPPED_CHUNKS_PER_GROUP)

            run("aaaaaaa", after_group=drain_some)
            for stream in streams:
                stream.finish()

        pl.run_scoped(mix_while_loading,
                      *_WeightStream.staging(w_up_hbm, OVERLAPPED_LOAD_SLOTS, OVERLAPPED_CHUNK_BYTES),
                      *_WeightStream.staging(w_down_hbm, OVERLAPPED_LOAD_SLOTS, OVERLAPPED_CHUNK_BYTES))

    def x_copy(t, buf):
        return pltpu.make_async_copy(x_hbm.at[pl.ds(t * tile, tile), :], x_buf.at[buf], x_sem.at[buf])

    def out_copy(t, buf):
        return pltpu.make_async_copy(o_buf.at[buf], out_hbm.at[pl.ds(t * tile, tile), :], o_sem.at[buf])

    assert n_tiles >= 4

    step, slot = 0, 0
    x_copy(0, 0).start()
    x_copy(1, 1).start()
    x_copy(0, 0).wait()
    first_step()

    def steady_pass(t, carry):
        nonlocal step, slot
        step, slot = t, t % 2
        pl.when(t + 1 < n_tiles)(lambda: x_copy(t + 1, 1 - slot).start())
        x_copy(t, slot).wait()
        pl.when(t >= 4)(lambda: out_copy(t - 4, slot).wait())
        run("bc" "ab" "aab" "ab" "b" "aa" "bb" "ab")
        pl.when(t >= 2)(lambda: out_copy(t - 2, slot).start())
        return carry

    lax.fori_loop(1, n_tiles, steady_pass, 0)

    for t, stages in ((n_tiles, "bc" "bbbbbbb"), (n_tiles + 1, "c")):
        step, slot = t, t % 2
        out_copy(t - 4, slot).wait()
        run(stages)
        out_copy(t - 2, slot).start()
    for t in (n_tiles - 2, n_tiles - 1):
        out_copy(t, t % 2).wait()


def _resident(a):
    del a
    return pl.BlockSpec(memory_space=pltpu.VMEM)


def kernel(x, norm1_g, w_in, conv_w, gmlp_ln_g, gmlp_ln_b, gmlp_ws, gmlp_bs, out_norm_conv_g,
           out_norm_gmlp_g, w_out, norm2_g, w_up, w_down, final_g):
    depth = w_in.shape[0]
    batch, seq, d_model = x.shape
    c = conv_w.shape[1]
    gw = gmlp_ln_g.shape[1]
    head_dim = gw // GMLP_HEADS
    assert depth == 1, "one fused block per call"
    assert seq % TILE == 0 and TILE % GMLP_BLOCK == 0
    assert head_dim * HEADS_PER_VREG == LANES
    assert w_up.shape[2] % N_FF_CHUNKS == 0
    tiles_per_batch = seq // TILE
    n_tiles = batch * tiles_per_batch
    bf16 = jnp.bfloat16
    row = lambda a: a.reshape(1, -1)
    bias = jnp.repeat(jnp.transpose(gmlp_bs[0]), head_dim, axis=1)

    in_hbm = pl.BlockSpec(memory_space=pl.ANY)
    operands_and_specs = (
        (x.reshape(batch * seq, d_model), in_hbm),
        (row(norm1_g[0]), None), (w_in[0], in_hbm), (jnp.transpose(conv_w[0]), None),
        (row(gmlp_ln_g[0]), None), (row(gmlp_ln_b[0]), None), (gmlp_ws[0], None), (bias, None),
        (row(out_norm_conv_g[0]), None), (row(out_norm_gmlp_g[0]), None), (w_out[0], in_hbm),
        (row(norm2_g[0]), None), (w_up[0], in_hbm), (w_down[0], in_hbm), (row(final_g), None),
    )
    operands = [a for a, _ in operands_and_specs]
    in_specs = [_resident(a) if spec is None else spec for a, spec in operands_and_specs]
    out = pl.pallas_call(
        functools.partial(_block_kernel, n_tiles=n_tiles, tiles_per_batch=tiles_per_batch),
        in_specs=in_specs,
        out_specs=pl.BlockSpec(memory_space=pl.ANY),
        out_shape=jax.ShapeDtypeStruct((batch * seq, d_model), x.dtype),
        scratch_shapes=[
            pltpu.VMEM(w_in.shape[1:], bf16),
            pltpu.VMEM(w_out.shape[1:], bf16),
            pltpu.VMEM(w_up.shape[1:], bf16),
            pltpu.VMEM(w_down.shape[1:], bf16),
            pltpu.VMEM((GMLP_HEADS // HEADS_PER_VREG, GMLP_BLOCK, HEADS_PER_VREG * GMLP_BLOCK), bf16),
            pltpu.VMEM((TILE + SUBLANES, c), jnp.float32),
            pltpu.VMEM((TILE, gw), jnp.float32),
            pltpu.VMEM((TILE, d_model), jnp.float32),
            pltpu.VMEM((TILE, d_model), bf16),
            pltpu.VMEM((TILE, LANES), jnp.float32),
            pltpu.VMEM((TILE, d_model), jnp.float32),
            pltpu.VMEM((TILE, LANES), jnp.float32),
            pltpu.VMEM((2, TILE, d_model), jnp.float32),
            pltpu.VMEM((2, TILE, d_model), jnp.float32),
            pltpu.SemaphoreType.DMA((2,)),
            pltpu.SemaphoreType.DMA((2,)),
        ],
        compiler_params=pltpu.CompilerParams(vmem_limit_bytes=VMEM_LIMIT_BYTES),
        name="hybrid_block",
    )(*operands)
    return out.reshape(batch, seq, d_model)
```

```python
import functools

import jax
import jax.numpy as jnp
from jax import lax
from jax.experimental import pallas as pl
from jax.experimental.pallas import tpu as pltpu

EPS = 1e-6
CHUNK = 64
GMLP_BLOCK = 128
GMLP_HEADS = 8
LANES = 128
SUBLANES = 8
HEADS_PER_VREG = 2

TILE = 512
N_FF_CHUNKS = 4
VMEM_LIMIT_BYTES = 60000 * 1024

WEIGHT_CHUNK_BYTES = 1024 * 1024
WEIGHT_LOAD_SLOTS = 8
OVERLAPPED_LOAD_SLOTS = 5
OVERLAPPED_CHUNKS_PER_GROUP = 2


def _inv_rms(x):
    return lax.rsqrt(jnp.mean(x * x, axis=-1, keepdims=True) + EPS)


def _rms(x, g):
    return x * _inv_rms(x) * g


class _WeightStream:
    def __init__(self, src_hbm, dst_ref, stage_ref, sem_ref):
        self.src_hbm, self.dst_ref, self.stage_ref, self.sem_ref = src_hbm, dst_ref, stage_ref, sem_ref
        self.slots, self.rows, _ = stage_ref.shape
        assert src_hbm.shape[0] % self.rows == 0
        self.n_chunks = src_hbm.shape[0] // self.rows
        self.done = 0

    @staticmethod
    def staging(src_hbm, slots):
        n_cols = src_hbm.shape[1]
        rows = 1 << ((WEIGHT_CHUNK_BYTES // (n_cols * 4)).bit_length() - 1)
        return (pltpu.VMEM((slots, rows, n_cols), jnp.float32), pltpu.SemaphoreType.DMA((slots,)))

    def _copy(self, i, slot):
        return pltpu.make_async_copy(self.src_hbm.at[pl.ds(i * self.rows, self.rows), :],
                                     self.stage_ref.at[slot], self.sem_ref.at[slot])

    def start(self):
        for i in range(min(self.slots - 1, self.n_chunks)):
            self._copy(i, i).start()

    def advance(self, count):
        first, last = self.done, min(self.done + count, self.n_chunks)

        def step(i, carry):
            ahead = i + self.slots - 1

            @pl.when(ahead < self.n_chunks)
            def _():
                self._copy(ahead, ahead % self.slots).start()

            slot = i % self.slots
            self._copy(i, slot).wait()
            self.dst_ref[pl.ds(pl.multiple_of(i * self.rows, self.rows), self.rows), :] = (
                self.stage_ref[slot].astype(jnp.bfloat16))
            return carry

        lax.fori_loop(first, last, step, 0)
        self.done = last

    def finish(self):
        self.advance(self.n_chunks - self.done)


def _load_weight_as_bf16(src_hbm, dst_ref):
    def body(stage_ref, sem_ref):
        stream = _WeightStream(src_hbm, dst_ref, stage_ref, sem_ref)
        stream.start()
        stream.finish()

    pl.run_scoped(body, *_WeightStream.staging(src_hbm, WEIGHT_LOAD_SLOTS))


def _block_kernel(x_hbm, g1_ref, w_in_hbm, conv_w_ref, ln_g_ref, ln_b_ref, ws_ref, bias_ref,
                  gc_ref, gg_ref, w_out_hbm, g2_ref, w_up_hbm, w_down_hbm, gf_ref,
                  out_hbm, w_in_ref, w_out_ref, w_up_ref, w_down_ref,
                  ws_pair_ref, zbuf_ref, yb_ref, x1_ref, h2_ref, rr2_ref, pre_ref, ssq_ref,
                  x_buf, o_buf, x_sem, o_sem, *, n_tiles, tiles_per_batch):
    f32, bf16 = jnp.float32, jnp.bfloat16
    tile = TILE
    c = conv_w_ref.shape[1]
    gw = ln_g_ref.shape[1]
    ff_chunk = w_up_ref.shape[1] // N_FF_CHUNKS
    n_blocks = tile // GMLP_BLOCK
    n_pairs = GMLP_HEADS // HEADS_PER_VREG
    step = slot = None

    def prepare():
        qi = lax.broadcasted_iota(jnp.int32, (GMLP_BLOCK, GMLP_BLOCK), 0) // CHUNK
        kj = lax.broadcasted_iota(jnp.int32, (GMLP_BLOCK, GMLP_BLOCK), 1) // CHUNK
        causal = kj <= qi
        for h in range(GMLP_HEADS):
            g, r = divmod(h, HEADS_PER_VREG)
            ws_pair_ref[g, :, r * GMLP_BLOCK:(r + 1) * GMLP_BLOCK] = (
                jnp.where(causal, ws_ref[h], 0.0).astype(bf16))
        zbuf_ref[0:SUBLANES, :] = jnp.zeros((SUBLANES, c), f32)
        pre_ref[...] = jnp.zeros(pre_ref.shape, f32)
        ssq_ref[...] = jnp.zeros(ssq_ref.shape, f32)

    def stage_a():
        x = x_buf[slot]
        h = _rms(x, g1_ref[...]).astype(bf16)
        yield

        def in_proj(lo, width):
            return jnp.dot(h, w_in_ref[:, lo:lo + width], preferred_element_type=f32)

        z = in_proj(c, c) * in_proj(2 * c, c)
        zbuf_ref[SUBLANES:SUBLANES + tile, :] = z
        cw = conv_w_ref[...]
        conv = (zbuf_ref[SUBLANES - 2:SUBLANES - 2 + tile, :] * cw[0:1, :]
                + zbuf_ref[SUBLANES - 1:SUBLANES - 1 + tile, :] * cw[1:2, :]
                + z * cw[2:3, :])
        next_starts_sequence = (step + 1) % tiles_per_batch == 0
        zbuf_ref[0:SUBLANES, :] = jnp.where(next_starts_sequence, 0.0,
                                            zbuf_ref[tile:tile + SUBLANES, :])
        yield
        y_a = _rms(in_proj(0, c) * conv, gc_ref[...]).astype(bf16)
        yield

        v = in_proj(3 * c + gw, gw)
        mu = jnp.mean(v, axis=-1, keepdims=True)
        vc = v - mu
        vn = vc * lax.rsqrt(jnp.mean(vc * vc, axis=-1, keepdims=True) + EPS)
        vn = vn * ln_g_ref[...] + ln_b_ref[...]
        yield
        lane = lax.broadcasted_iota(jnp.int32, (tile, LANES), 1)
        low_head = lane < (LANES // HEADS_PER_VREG)
        for g in range(n_pairs):
            vg = vn[:, g * LANES:(g + 1) * LANES]
            v_lo = jnp.where(low_head, vg, 0.0).astype(bf16)
            v_hi = jnp.where(low_head, 0.0, vg).astype(bf16)
            rhs = jnp.concatenate(
                [jnp.concatenate([v_lo[n * GMLP_BLOCK:(n + 1) * GMLP_BLOCK, :]
                                  for n in range(n_blocks)], axis=1),
                 jnp.concatenate([v_hi[n * GMLP_BLOCK:(n + 1) * GMLP_BLOCK, :]
                                  for n in range(n_blocks)], axis=1)],
                axis=0)
            mixed = jnp.dot(ws_pair_ref[g], rhs, preferred_element_type=f32)
            for n in range(n_blocks):
                yb_ref[n * GMLP_BLOCK:(n + 1) * GMLP_BLOCK, g * LANES:(g + 1) * LANES] = (
                    mixed[:, n * LANES:(n + 1) * LANES])
        yield
        u = in_proj(3 * c, gw)
        bias = jnp.concatenate([bias_ref[...]] * n_blocks, axis=0)
        y_b = _rms(u * (yb_ref[...] + bias), gg_ref[...]).astype(bf16)
        yield

        y = jnp.concatenate([y_a, y_b], axis=-1)
        x1 = x + jnp.dot(y, w_out_ref[...], preferred_element_type=f32)
        x1_ref[...] = x1
        h2_ref[...] = (x1 * g2_ref[...]).astype(bf16)
        r2 = _inv_rms(x1)
        rr2_ref[...] = jnp.broadcast_to(r2 * r2, rr2_ref.shape)
        yield

    def stage_b():
        rr2 = jnp.concatenate([rr2_ref[...]] * (ff_chunk // LANES), axis=1)
        acc = None
        for k in range(N_FF_CHUNKS):
            cols = slice(k * ff_chunk, (k + 1) * ff_chunk)
            a = jnp.maximum(
                jnp.dot(h2_ref[...], w_up_ref[:, cols], preferred_element_type=f32), 0.0)
            a = (a * a * rr2).astype(bf16)
            yield
            if acc is None:
                acc = x1_ref[...]
            acc = acc + jnp.dot(a, w_down_ref[cols, :], preferred_element_type=f32)
            if k == N_FF_CHUNKS - 1:
                pre_ref[...] = acc
                sq = acc * acc
                ssq_ref[...] = sum(sq[:, j:j + LANES] for j in range(0, sq.shape[1], LANES))
            yield

    def stage_c():
        pre = pre_ref[...]
        ms = jnp.sum(ssq_ref[...], axis=-1, keepdims=True) * (1.0 / pre.shape[1])
        o_buf[slot] = pre * lax.rsqrt(ms + EPS) * gf_ref[...]

    def run(order, after_group=None):
        b_groups = [i for i, name in enumerate(order) if name == "b"]
        if "a" in order and b_groups:
            assert order.rindex("a") > b_groups[0::2][-1]
        if "c" in order and b_groups:
            assert order.index("c") < b_groups[-1]
        gens = {}
        for name in order:
            if name == "c":
                stage_c()
                continue
            if name not in gens:
                gens[name] = {"a": stage_a, "b": stage_b}[name]()
            next(gens[name])
            if after_group is not None:
                after_group()
        for gen in gens.values():
            assert next(gen, "done") == "done"

    def first_step():
        _load_weight_as_bf16(w_in_hbm, w_in_ref)
        _load_weight_as_bf16(w_out_hbm, w_out_ref)
        prepare()

        def mix_while_loading(up_stage, up_sem, down_stage, down_sem):
            streams = (_WeightStream(w_up_hbm, w_up_ref, up_stage, up_sem),
                       _WeightStream(w_down_hbm, w_down_ref, down_stage, down_sem))
            for stream in streams:
                stream.start()

            def drain_some():
                for stream in streams:
                    stream.advance(OVERLAPPED_CHUNKS_PER_GROUP)

            run("aaaaaaa", after_group=drain_some)
            for stream in streams:
                stream.finish()

        pl.run_scoped(mix_while_loading,
                      *_WeightStream.staging(w_up_hbm, OVERLAPPED_LOAD_SLOTS),
                      *_WeightStream.staging(w_down_hbm, OVERLAPPED_LOAD_SLOTS))

    def x_copy(t, buf):
        return pltpu.make_async_copy(x_hbm.at[pl.ds(t * tile, tile), :], x_buf.at[buf], x_sem.at[buf])

    def out_copy(t, buf):
        return pltpu.make_async_copy(o_buf.at[buf], out_hbm.at[pl.ds(t * tile, tile), :], o_sem.at[buf])

    assert n_tiles >= 4

    step, slot = 0, 0
    x_copy(0, 0).start()
    x_copy(1, 1).start()
    x_copy(0, 0).wait()
    first_step()

    def steady_pass(t, carry):
        nonlocal step, slot
        step, slot = t, t % 2
        pl.when(t + 1 < n_tiles)(lambda: x_copy(t + 1, 1 - slot).start())
        x_copy(t, slot).wait()
        pl.when(t >= 4)(lambda: out_copy(t - 4, slot).wait())
        run("bc" "ab" "aab" "ab" "b" "aa" "bb" "ab")
        pl.when(t >= 2)(lambda: out_copy(t - 2, slot).start())
        return carry

    lax.fori_loop(1, n_tiles, steady_pass, 0)

    for t, stages in ((n_tiles, "bc" "bbbbbbb"), (n_tiles + 1, "c")):
        step, slot = t, t % 2
        out_copy(t - 4, slot).wait()
        run(stages)
        out_copy(t - 2, slot).start()
    for t in (n_tiles - 2, n_tiles - 1):
        out_copy(t, t % 2).wait()


def _resident(a):
    del a
    return pl.BlockSpec(memory_space=pltpu.VMEM)


def kernel(x, norm1_g, w_in, conv_w, gmlp_ln_g, gmlp_ln_b, gmlp_ws, gmlp_bs, out_norm_conv_g,
           out_norm_gmlp_g, w_out, norm2_g, w_up, w_down, final_g):
    depth = w_in.shape[0]
    batch, seq, d_model = x.shape
    c = conv_w.shape[1]
    gw = gmlp_ln_g.shape[1]
    head_dim = gw // GMLP_HEADS
    assert depth == 1, "one fused block per call"
    assert seq % TILE == 0 and TILE % GMLP_BLOCK == 0
    assert head_dim * HEADS_PER_VREG == LANES
    assert w_up.shape[2] % N_FF_CHUNKS == 0
    tiles_per_batch = seq // TILE
    n_tiles = batch * tiles_per_batch
    bf16 = jnp.bfloat16
    row = lambda a: a.reshape(1, -1)
    bias = jnp.repeat(jnp.transpose(gmlp_bs[0]), head_dim, axis=1)

    in_hbm = pl.BlockSpec(memory_space=pl.ANY)
    operands_and_specs = (
        (x.reshape(batch * seq, d_model), in_hbm),
        (row(norm1_g[0]), None), (w_in[0], in_hbm), (jnp.transpose(conv_w[0]), None),
        (row(gmlp_ln_g[0]), None), (row(gmlp_ln_b[0]), None), (gmlp_ws[0], None), (bias, None),
        (row(out_norm_conv_g[0]), None), (row(out_norm_gmlp_g[0]), None), (w_out[0], in_hbm),
        (row(norm2_g[0]), None), (w_up[0], in_hbm), (w_down[0], in_hbm), (row(final_g), None),
    )
    operands = [a for a, _ in operands_and_specs]
    in_specs = [_resident(a) if spec is None else spec for a, spec in operands_and_specs]
    out = pl.pallas_call(
        functools.partial(_block_kernel, n_tiles=n_tiles, tiles_per_batch=tiles_per_batch),
        in_specs=in_specs,
        out_specs=pl.BlockSpec(memory_space=pl.ANY),
        out_shape=jax.ShapeDtypeStruct((batch * seq, d_model), x.dtype),
        scratch_shapes=[
            pltpu.VMEM(w_in.shape[1:], bf16),
            pltpu.VMEM(w_out.shape[1:], bf16),
            pltpu.VMEM(w_up.shape[1:], bf16),
            pltpu.VMEM(w_down.shape[1:], bf16),
            pltpu.VMEM((GMLP_HEADS // HEADS_PER_VREG, GMLP_BLOCK, HEADS_PER_VREG * GMLP_BLOCK), bf16),
            pltpu.VMEM((TILE + SUBLANES, c), jnp.float32),
            pltpu.VMEM((TILE, gw), jnp.float32),
            pltpu.VMEM((TILE, d_model), jnp.float32),
            pltpu.VMEM((TILE, d_model), bf16),
            pltpu.VMEM((TILE, LANES), jnp.float32),
            pltpu.VMEM((TILE, d_model), jnp.float32),
            pltpu.VMEM((TILE, LANES), jnp.float32),
            pltpu.VMEM((2, TILE, d_model), jnp.float32),
            pltpu.VMEM((2, TILE, d_model), jnp.float32),
            pltpu.SemaphoreType.DMA((2,)),
            pltpu.SemaphoreType.DMA((2,)),
        ],
        compiler_params=pltpu.CompilerParams(vmem_limit_bytes=VMEM_LIMIT_BYTES),
        name="hybrid_block",
    )(*operands)
    return out.reshape(batch, seq, d_model)
```

```python
import functools

import jax
import jax.numpy as jnp
from jax import lax
from jax.experimental import pallas as pl
from jax.experimental.pallas import tpu as pltpu

EPS = 1e-6
CHUNK = 64
GMLP_BLOCK = 128
GMLP_HEADS = 8
LANES = 128
SUBLANES = 8
HEADS_PER_VREG = 2

TILE = 512
N_FF_CHUNKS = 4
VMEM_LIMIT_BYTES = 60000 * 1024

WEIGHT_CHUNK_BYTES = 1024 * 1024
WEIGHT_LOAD_SLOTS = 8
OVERLAPPED_LOAD_SLOTS = 5
OVERLAPPED_CHUNKS_PER_GROUP = 2


def _inv_rms(x):
    return lax.rsqrt(jnp.mean(x * x, axis=-1, keepdims=True) + EPS)


def _rms(x, g):
    return x * _inv_rms(x) * g


class _WeightStream:
    def __init__(self, src_hbm, dst_ref, stage_ref, sem_ref):
        self.src_hbm, self.dst_ref, self.stage_ref, self.sem_ref = src_hbm, dst_ref, stage_ref, sem_ref
        self.slots, self.rows, _ = stage_ref.shape
        assert src_hbm.shape[0] % self.rows == 0
        self.n_chunks = src_hbm.shape[0] // self.rows
        self.done = 0

    @staticmethod
    def staging(src_hbm, slots):
        n_cols = src_hbm.shape[1]
        rows = 1 << ((WEIGHT_CHUNK_BYTES // (n_cols * 4)).bit_length() - 1)
        return (pltpu.VMEM((slots, rows, n_cols), jnp.float32), pltpu.SemaphoreType.DMA((slots,)))

    def _copy(self, i, slot):
        return pltpu.make_async_copy(self.src_hbm.at[pl.ds(i * self.rows, self.rows), :],
                                     self.stage_ref.at[slot], self.sem_ref.at[slot])

    def start(self):
        for i in range(min(self.slots - 1, self.n_chunks)):
            self._copy(i, i).start()

    def advance(self, count):
        first, last = self.done, min(self.done + count, self.n_chunks)

        def step(i, carry):
            ahead = i + self.slots - 1

            @pl.when(ahead < self.n_chunks)
            def _():
                self._copy(ahead, ahead % self.slots).start()

            slot = i % self.slots
            self._copy(i, slot).wait()
            self.dst_ref[pl.ds(pl.multiple_of(i * self.rows, self.rows), self.rows), :] = (
                self.stage_ref[slot].astype(jnp.bfloat16))
            return carry

        lax.fori_loop(first, last, step, 0)
        self.done = last

    def finish(self):
        self.advance(self.n_chunks - self.done)


def _load_weight_as_bf16(src_hbm, dst_ref, meanwhile=None):
    def body(stage_ref, sem_ref):
        stream = _WeightStream(src_hbm, dst_ref, stage_ref, sem_ref)
        stream.start()
        if meanwhile is not None:
            meanwhile()
        stream.finish()

    pl.run_scoped(body, *_WeightStream.staging(src_hbm, WEIGHT_LOAD_SLOTS))


def _block_kernel(x_hbm, g1_ref, w_in_hbm, conv_w_ref, ln_g_ref, ln_b_ref, ws_ref, bias_ref,
                  gc_ref, gg_ref, w_out_hbm, g2_ref, w_up_hbm, w_down_hbm, gf_ref,
                  out_hbm, w_in_ref, w_out_ref, w_up_ref, w_down_ref,
                  ws_pair_ref, zbuf_ref, yb_ref, x1_ref, h2_ref, rr2_ref, pre_ref, ssq_ref,
                  x_buf, o_buf, x_sem, o_sem, *, n_tiles, tiles_per_batch):
    f32, bf16 = jnp.float32, jnp.bfloat16
    tile = TILE
    c = conv_w_ref.shape[1]
    gw = ln_g_ref.shape[1]
    ff_chunk = w_up_ref.shape[1] // N_FF_CHUNKS
    n_blocks = tile // GMLP_BLOCK
    n_pairs = GMLP_HEADS // HEADS_PER_VREG
    step = slot = None

    def prepare():
        qi = lax.broadcasted_iota(jnp.int32, (GMLP_BLOCK, GMLP_BLOCK), 0) // CHUNK
        kj = lax.broadcasted_iota(jnp.int32, (GMLP_BLOCK, GMLP_BLOCK), 1) // CHUNK
        causal = kj <= qi
        for h in range(GMLP_HEADS):
            g, r = divmod(h, HEADS_PER_VREG)
            ws_pair_ref[g, :, r * GMLP_BLOCK:(r + 1) * GMLP_BLOCK] = (
                jnp.where(causal, ws_ref[h], 0.0).astype(bf16))
        zbuf_ref[0:SUBLANES, :] = jnp.zeros((SUBLANES, c), f32)
        pre_ref[...] = jnp.zeros(pre_ref.shape, f32)
        ssq_ref[...] = jnp.zeros(ssq_ref.shape, f32)

    def stage_a():
        x = x_buf[slot]
        h = _rms(x, g1_ref[...]).astype(bf16)
        yield

        def in_proj(lo, width):
            return jnp.dot(h, w_in_ref[:, lo:lo + width], preferred_element_type=f32)

        z = in_proj(c, c) * in_proj(2 * c, c)
        zbuf_ref[SUBLANES:SUBLANES + tile, :] = z
        cw = conv_w_ref[...]
        conv = (zbuf_ref[SUBLANES - 2:SUBLANES - 2 + tile, :] * cw[0:1, :]
                + zbuf_ref[SUBLANES - 1:SUBLANES - 1 + tile, :] * cw[1:2, :]
                + z * cw[2:3, :])
        next_starts_sequence = (step + 1) % tiles_per_batch == 0
        zbuf_ref[0:SUBLANES, :] = jnp.where(next_starts_sequence, 0.0,
                                            zbuf_ref[tile:tile + SUBLANES, :])
        yield
        y_a = _rms(in_proj(0, c) * conv, gc_ref[...]).astype(bf16)
        yield

        v = in_proj(3 * c + gw, gw)
        mu = jnp.mean(v, axis=-1, keepdims=True)
        vc = v - mu
        vn = vc * lax.rsqrt(jnp.mean(vc * vc, axis=-1, keepdims=True) + EPS)
        vn = vn * ln_g_ref[...] + ln_b_ref[...]
        yield
        lane = lax.broadcasted_iota(jnp.int32, (tile, LANES), 1)
        low_head = lane < (LANES // HEADS_PER_VREG)
        for g in range(n_pairs):
            vg = vn[:, g * LANES:(g + 1) * LANES]
            v_lo = jnp.where(low_head, vg, 0.0).astype(bf16)
            v_hi = jnp.where(low_head, 0.0, vg).astype(bf16)
            rhs = jnp.concatenate(
                [jnp.concatenate([v_lo[n * GMLP_BLOCK:(n + 1) * GMLP_BLOCK, :]
                                  for n in range(n_blocks)], axis=1),
                 jnp.concatenate([v_hi[n * GMLP_BLOCK:(n + 1) * GMLP_BLOCK, :]
                                  for n in range(n_blocks)], axis=1)],
                axis=0)
            mixed = jnp.dot(ws_pair_ref[g], rhs, preferred_element_type=f32)
            for n in range(n_blocks):
                yb_ref[n * GMLP_BLOCK:(n + 1) * GMLP_BLOCK, g * LANES:(g + 1) * LANES] = (
                    mixed[:, n * LANES:(n + 1) * LANES])
        yield
        u = in_proj(3 * c, gw)
        bias = jnp.concatenate([bias_ref[...]] * n_blocks, axis=0)
        y_b = _rms(u * (yb_ref[...] + bias), gg_ref[...]).astype(bf16)
        yield

        y = jnp.concatenate([y_a, y_b], axis=-1)
        x1 = x + jnp.dot(y, w_out_ref[...], preferred_element_type=f32)
        x1_ref[...] = x1
        h2_ref[...] = (x1 * g2_ref[...]).astype(bf16)
        r2 = _inv_rms(x1)
        rr2_ref[...] = jnp.broadcast_to(r2 * r2, rr2_ref.shape)
        yield

    def stage_b():
        rr2 = jnp.concatenate([rr2_ref[...]] * (ff_chunk // LANES), axis=1)
        acc = None
        for k in range(N_FF_CHUNKS):
            cols = slice(k * ff_chunk, (k + 1) * ff_chunk)
            a = jnp.maximum(
                jnp.dot(h2_ref[...], w_up_ref[:, cols], preferred_element_type=f32), 0.0)
            a = (a * a * rr2).astype(bf16)
            yield
            if acc is None:
                acc = x1_ref[...]
            acc = acc + jnp.dot(a, w_down_ref[cols, :], preferred_element_type=f32)
            if k == N_FF_CHUNKS - 1:
                pre_ref[...] = acc
                sq = acc * acc
                ssq_ref[...] = sum(sq[:, j:j + LANES] for j in range(0, sq.shape[1], LANES))
            yield

    def stage_c():
        pre = pre_ref[...]
        ms = jnp.sum(ssq_ref[...], axis=-1, keepdims=True) * (1.0 / pre.shape[1])
        o_buf[slot] = pre * lax.rsqrt(ms + EPS) * gf_ref[...]

    def run(order, after_group=None):
        b_groups = [i for i, name in enumerate(order) if name == "b"]
        if "a" in order and b_groups:
            assert order.rindex("a") > b_groups[0::2][-1]
        if "c" in order and b_groups:
            assert order.index("c") < b_groups[-1]
        gens = {}
        for name in order:
            if name == "c":
                stage_c()
                continue
            if name not in gens:
                gens[name] = {"a": stage_a, "b": stage_b}[name]()
            next(gens[name])
            if after_group is not None:
                after_group()
        for gen in gens.values():
            assert next(gen, "done") == "done"

    def first_step():
        _load_weight_as_bf16(w_in_hbm, w_in_ref, meanwhile=prepare)
        _load_weight_as_bf16(w_out_hbm, w_out_ref)

        def mix_while_loading(up_stage, up_sem, down_stage, down_sem):
            streams = (_WeightStream(w_up_hbm, w_up_ref, up_stage, up_sem),
                       _WeightStream(w_down_hbm, w_down_ref, down_stage, down_sem))
            for stream in streams:
                stream.start()

            def drain_some():
                for stream in streams:
                    stream.advance(OVERLAPPED_CHUNKS_PER_GROUP)

            run("aaaaaaa", after_group=drain_some)
            for stream in streams:
                stream.finish()

        pl.run_scoped(mix_while_loading,
                      *_WeightStream.staging(w_up_hbm, OVERLAPPED_LOAD_SLOTS),
                      *_WeightStream.staging(w_down_hbm, OVERLAPPED_LOAD_SLOTS))

    def x_copy(t, buf):
        return pltpu.make_async_copy(x_hbm.at[pl.ds(t * tile, tile), :], x_buf.at[buf], x_sem.at[buf])

    def out_copy(t, buf):
        return pltpu.make_async_copy(o_buf.at[buf], out_hbm.at[pl.ds(t * tile, tile), :], o_sem.at[buf])

    assert n_tiles >= 4

    step, slot = 0, 0
    x_copy(0, 0).start()
    x_copy(1, 1).start()
    x_copy(0, 0).wait()
    first_step()

    def steady_pass(t, carry):
        nonlocal step, slot
        step, slot = t, t % 2
        pl.when(t + 1 < n_tiles)(lambda: x_copy(t + 1, 1 - slot).start())
        x_copy(t, slot).wait()
        pl.when(t >= 4)(lambda: out_copy(t - 4, slot).wait())
        run("bc" "ab" "aab" "ab" "b" "aa" "bb" "ab")
        pl.when(t >= 2)(lambda: out_copy(t - 2, slot).start())
        return carry

    lax.fori_loop(1, n_tiles, steady_pass, 0)

    for t, stages in ((n_tiles, "bc" "bbbbbbb"), (n_tiles + 1, "c")):
        step, slot = t, t % 2
        out_copy(t - 4, slot).wait()
        run(stages)
        out_copy(t - 2, slot).start()
    for t in (n_tiles - 2, n_tiles - 1):
        out_copy(t, t % 2).wait()


def _resident(a):
    del a
    return pl.BlockSpec(memory_space=pltpu.VMEM)


def kernel(x, norm1_g, w_in, conv_w, gmlp_ln_g, gmlp_ln_b, gmlp_ws, gmlp_bs, out_norm_conv_g,
           out_norm_gmlp_g, w_out, norm2_g, w_up, w_down, final_g):
    depth = w_in.shape[0]
    batch, seq, d_model = x.shape
    c = conv_w.shape[1]
    gw = gmlp_ln_g.shape[1]
    head_dim = gw // GMLP_HEADS
    assert depth == 1, "one fused block per call"
    assert seq % TILE == 0 and TILE % GMLP_BLOCK == 0
    assert head_dim * HEADS_PER_VREG == LANES
    assert w_up.shape[2] % N_FF_CHUNKS == 0
    tiles_per_batch = seq // TILE
    n_tiles = batch * tiles_per_batch
    bf16 = jnp.bfloat16
    row = lambda a: a.reshape(1, -1)
    bias = jnp.repeat(jnp.transpose(gmlp_bs[0]), head_dim, axis=1)

    in_hbm = pl.BlockSpec(memory_space=pl.ANY)
    operands_and_specs = (
        (x.reshape(batch * seq, d_model), in_hbm),
        (row(norm1_g[0]), None), (w_in[0], in_hbm), (jnp.transpose(conv_w[0]), None),
        (row(gmlp_ln_g[0]), None), (row(gmlp_ln_b[0]), None), (gmlp_ws[0], None), (bias, None),
        (row(out_norm_conv_g[0]), None), (row(out_norm_gmlp_g[0]), None), (w_out[0], in_hbm),
        (row(norm2_g[0]), None), (w_up[0], in_hbm), (w_down[0], in_hbm), (row(final_g), None),
    )
    operands = [a for a, _ in operands_and_specs]
    in_specs = [_resident(a) if spec is None else spec for a, spec in operands_and_specs]
    out = pl.pallas_call(
        functools.partial(_block_kernel, n_tiles=n_tiles, tiles_per_batch=tiles_per_batch),
        in_specs=in_specs,
        out_specs=pl.BlockSpec(memory_space=pl.ANY),
        out_shape=jax.ShapeDtypeStruct((batch * seq, d_model), x.dtype),
        scratch_shapes=[
            pltpu.VMEM(w_in.shape[1:], bf16),
            pltpu.VMEM(w_out.shape[1:], bf16),
            pltpu.VMEM(w_up.shape[1:], bf16),
            pltpu.VMEM(w_down.shape[1:], bf16),
            pltpu.VMEM((GMLP_HEADS // HEADS_PER_VREG, GMLP_BLOCK, HEADS_PER_VREG * GMLP_BLOCK), bf16),
            pltpu.VMEM((TILE + SUBLANES, c), jnp.float32),
            pltpu.VMEM((TILE, gw), jnp.float32),
            pltpu.VMEM((TILE, d_model), jnp.float32),
            pltpu.VMEM((TILE, d_model), bf16),
            pltpu.VMEM((TILE, LANES), jnp.float32),
            pltpu.VMEM((TILE, d_model), jnp.float32),
            pltpu.VMEM((TILE, LANES), jnp.float32),
            pltpu.VMEM((2, TILE, d_model), jnp.float32),
            pltpu.VMEM((2, TILE, d_model), jnp.float32),
            pltpu.SemaphoreType.DMA((2,)),
            pltpu.SemaphoreType.DMA((2,)),
        ],
        compiler_params=pltpu.CompilerParams(vmem_limit_bytes=VMEM_LIMIT_BYTES),
        name="hybrid_block",
    )(*operands)
    return out.reshape(batch, seq, d_model)
```

```python
import functools

import jax
import jax.numpy as jnp
from jax import lax
from jax.experimental import pallas as pl
from jax.experimental.pallas import tpu as pltpu

EPS = 1e-6
CHUNK = 64
GMLP_BLOCK = 128
GMLP_HEADS = 8
LANES = 128
SUBLANES = 8
HEADS_PER_VREG = 2

TILE = 512
N_FF_CHUNKS = 4
VMEM_LIMIT_BYTES = 60000 * 1024

WEIGHT_CHUNK_BYTES = 1024 * 1024
WEIGHT_LOAD_SLOTS = 8


def _inv_rms(x):
    return lax.rsqrt(jnp.mean(x * x, axis=-1, keepdims=True) + EPS)


def _rms(x, g):
    return x * _inv_rms(x) * g


class _WeightStream:
    def __init__(self, src_hbm, dst_ref, stage_ref, sem_ref):
        self.src_hbm, self.dst_ref, self.stage_ref, self.sem_ref = src_hbm, dst_ref, stage_ref, sem_ref
        self.slots, self.rows, _ = stage_ref.shape
        assert src_hbm.shape[0] % self.rows == 0
        self.n_chunks = src_hbm.shape[0] // self.rows
        self.done = 0

    @staticmethod
    def staging(src_hbm, slots):
        n_cols = src_hbm.shape[1]
        rows = 1 << ((WEIGHT_CHUNK_BYTES // (n_cols * 4)).bit_length() - 1)
        return (pltpu.VMEM((slots, rows, n_cols), jnp.float32), pltpu.SemaphoreType.DMA((slots,)))

    def _copy(self, i, slot):
        return pltpu.make_async_copy(self.src_hbm.at[pl.ds(i * self.rows, self.rows), :],
                                     self.stage_ref.at[slot], self.sem_ref.at[slot])

    def start(self):
        for i in range(min(self.slots - 1, self.n_chunks)):
            self._copy(i, i).start()

    def advance(self, count):
        first, last = self.done, min(self.done + count, self.n_chunks)

        def step(i, carry):
            ahead = i + self.slots - 1

            @pl.when(ahead < self.n_chunks)
            def _():
                self._copy(ahead, ahead % self.slots).start()

            slot = i % self.slots
            self._copy(i, slot).wait()
            self.dst_ref[pl.ds(pl.multiple_of(i * self.rows, self.rows), self.rows), :] = (
                self.stage_ref[slot].astype(jnp.bfloat16))
            return carry

        lax.fori_loop(first, last, step, 0)
        self.done = last

    def finish(self):
        self.advance(self.n_chunks - self.done)


def _load_weight_as_bf16(src_hbm, dst_ref):
    def body(stage_ref, sem_ref):
        stream = _WeightStream(src_hbm, dst_ref, stage_ref, sem_ref)
        stream.start()
        stream.finish()

    pl.run_scoped(body, *_WeightStream.staging(src_hbm, WEIGHT_LOAD_SLOTS))


def _block_kernel(x_hbm, g1_ref, w_in_hbm, conv_w_ref, ln_g_ref, ln_b_ref, ws_ref, bias_ref,
                  gc_ref, gg_ref, w_out_hbm, g2_ref, w_up_hbm, w_down_hbm, gf_ref,
                  out_hbm, w_in_ref, w_out_ref, w_up_ref, w_down_ref,
                  ws_pair_ref, zbuf_ref, yb_ref, x1_ref, h2_ref, rr2_ref, pre_ref, ssq_ref,
                  x_buf, o_buf, x_sem, o_sem, *, n_tiles, tiles_per_batch):
    f32, bf16 = jnp.float32, jnp.bfloat16
    tile = TILE
    c = conv_w_ref.shape[1]
    gw = ln_g_ref.shape[1]
    ff_chunk = w_up_ref.shape[1] // N_FF_CHUNKS
    n_blocks = tile // GMLP_BLOCK
    n_pairs = GMLP_HEADS // HEADS_PER_VREG
    step = slot = None

    def prepare():
        qi = lax.broadcasted_iota(jnp.int32, (GMLP_BLOCK, GMLP_BLOCK), 0) // CHUNK
        kj = lax.broadcasted_iota(jnp.int32, (GMLP_BLOCK, GMLP_BLOCK), 1) // CHUNK
        causal = kj <= qi
        for h in range(GMLP_HEADS):
            g, r = divmod(h, HEADS_PER_VREG)
            ws_pair_ref[g, :, r * GMLP_BLOCK:(r + 1) * GMLP_BLOCK] = (
                jnp.where(causal, ws_ref[h], 0.0).astype(bf16))
        zbuf_ref[0:SUBLANES, :] = jnp.zeros((SUBLANES, c), f32)
        pre_ref[...] = jnp.zeros(pre_ref.shape, f32)
        ssq_ref[...] = jnp.zeros(ssq_ref.shape, f32)

    def stage_a():
        x = x_buf[slot]
        h = _rms(x, g1_ref[...]).astype(bf16)
        yield

        def in_proj(lo, width):
            return jnp.dot(h, w_in_ref[:, lo:lo + width], preferred_element_type=f32)

        z = in_proj(c, c) * in_proj(2 * c, c)
        zbuf_ref[SUBLANES:SUBLANES + tile, :] = z
        cw = conv_w_ref[...]
        conv = (zbuf_ref[SUBLANES - 2:SUBLANES - 2 + tile, :] * cw[0:1, :]
                + zbuf_ref[SUBLANES - 1:SUBLANES - 1 + tile, :] * cw[1:2, :]
                + z * cw[2:3, :])
        next_starts_sequence = (step + 1) % tiles_per_batch == 0
        zbuf_ref[0:SUBLANES, :] = jnp.where(next_starts_sequence, 0.0,
                                            zbuf_ref[tile:tile + SUBLANES, :])
        yield
        y_a = _rms(in_proj(0, c) * conv, gc_ref[...]).astype(bf16)
        yield

        v = in_proj(3 * c + gw, gw)
        mu = jnp.mean(v, axis=-1, keepdims=True)
        vc = v - mu
        vn = vc * lax.rsqrt(jnp.mean(vc * vc, axis=-1, keepdims=True) + EPS)
        vn = vn * ln_g_ref[...] + ln_b_ref[...]
        yield
        lane = lax.broadcasted_iota(jnp.int32, (tile, LANES), 1)
        low_head = lane < (LANES // HEADS_PER_VREG)
        for g in range(n_pairs):
            vg = vn[:, g * LANES:(g + 1) * LANES]
            v_lo = jnp.where(low_head, vg, 0.0).astype(bf16)
            v_hi = jnp.where(low_head, 0.0, vg).astype(bf16)
            rhs = jnp.concatenate(
                [jnp.concatenate([v_lo[n * GMLP_BLOCK:(n + 1) * GMLP_BLOCK, :]
                                  for n in range(n_blocks)], axis=1),
                 jnp.concatenate([v_hi[n * GMLP_BLOCK:(n + 1) * GMLP_BLOCK, :]
                                  for n in range(n_blocks)], axis=1)],
                axis=0)
            mixed = jnp.dot(ws_pair_ref[g], rhs, preferred_element_type=f32)
            for n in range(n_blocks):
                yb_ref[n * GMLP_BLOCK:(n + 1) * GMLP_BLOCK, g * LANES:(g + 1) * LANES] = (
                    mixed[:, n * LANES:(n + 1) * LANES])
        yield
        u = in_proj(3 * c, gw)
        bias = jnp.concatenate([bias_ref[...]] * n_blocks, axis=0)
        y_b = _rms(u * (yb_ref[...] + bias), gg_ref[...]).astype(bf16)
        yield

        y = jnp.concatenate([y_a, y_b], axis=-1)
        x1 = x + jnp.dot(y, w_out_ref[...], preferred_element_type=f32)
        x1_ref[...] = x1
        h2_ref[...] = (x1 * g2_ref[...]).astype(bf16)
        r2 = _inv_rms(x1)
        rr2_ref[...] = jnp.broadcast_to(r2 * r2, rr2_ref.shape)
        yield

    def stage_b():
        rr2 = jnp.concatenate([rr2_ref[...]] * (ff_chunk // LANES), axis=1)
        acc = None
        for k in range(N_FF_CHUNKS):
            cols = slice(k * ff_chunk, (k + 1) * ff_chunk)
            a = jnp.maximum(
                jnp.dot(h2_ref[...], w_up_ref[:, cols], preferred_element_type=f32), 0.0)
            a = (a * a * rr2).astype(bf16)
            yield
            if acc is None:
                acc = x1_ref[...]
            acc = acc + jnp.dot(a, w_down_ref[cols, :], preferred_element_type=f32)
            if k == N_FF_CHUNKS - 1:
                pre_ref[...] = acc
                sq = acc * acc
                ssq_ref[...] = sum(sq[:, j:j + LANES] for j in range(0, sq.shape[1], LANES))
            yield

    def stage_c():
        pre = pre_ref[...]
        ms = jnp.sum(ssq_ref[...], axis=-1, keepdims=True) * (1.0 / pre.shape[1])
        o_buf[slot] = pre * lax.rsqrt(ms + EPS) * gf_ref[...]

    def run(order, after_group=None):
        b_groups = [i for i, name in enumerate(order) if name == "b"]
        if "a" in order and b_groups:
            assert order.rindex("a") > b_groups[0::2][-1]
        if "c" in order and b_groups:
            assert order.index("c") < b_groups[-1]
        gens = {}
        for name in order:
            if name == "c":
                stage_c()
                continue
            if name not in gens:
                gens[name] = {"a": stage_a, "b": stage_b}[name]()
            next(gens[name])
            if after_group is not None:
                after_group()
        for gen in gens.values():
            assert next(gen, "done") == "done"

    def first_step():
        _load_weight_as_bf16(w_in_hbm, w_in_ref)
        prepare()

        overlapped = ((w_out_hbm, w_out_ref, 2, 1),
                      (w_up_hbm, w_up_ref, 5, 2),
                      (w_down_hbm, w_down_ref, 4, 2))

        def mix_while_loading(*staging):
            streams = [_WeightStream(src, dst, *staging[2 * i:2 * i + 2])
                       for i, (src, dst, _, _) in enumerate(overlapped)]
            for stream in streams:
                stream.start()

            def drain_some():
                for stream, (_, _, _, per_group) in zip(streams, overlapped):
                    stream.advance(per_group)

            run("aaaaaaa", after_group=drain_some)
            for stream in streams:
                stream.finish()

        assert 6 * overlapped[0][3] * (WEIGHT_CHUNK_BYTES // 4) >= w_out_ref.shape[0] * w_out_ref.shape[1]
        pl.run_scoped(mix_while_loading,
                      *[s for src, _, slots, _ in overlapped
                        for s in _WeightStream.staging(src, slots)])

    def x_copy(t, buf):
        return pltpu.make_async_copy(x_hbm.at[pl.ds(t * tile, tile), :], x_buf.at[buf], x_sem.at[buf])

    def out_copy(t, buf):
        return pltpu.make_async_copy(o_buf.at[buf], out_hbm.at[pl.ds(t * tile, tile), :], o_sem.at[buf])

    assert n_tiles >= 4

    step, slot = 0, 0
    x_copy(0, 0).start()
    x_copy(1, 1).start()
    x_copy(0, 0).wait()
    first_step()

    def steady_pass(t, carry):
        nonlocal step, slot
        step, slot = t, t % 2
        pl.when(t + 1 < n_tiles)(lambda: x_copy(t + 1, 1 - slot).start())
        x_copy(t, slot).wait()
        pl.when(t >= 4)(lambda: out_copy(t - 4, slot).wait())
        run("bc" "ab" "aab" "ab" "b" "aa" "bb" "ab")
        pl.when(t >= 2)(lambda: out_copy(t - 2, slot).start())
        return carry

    lax.fori_loop(1, n_tiles, steady_pass, 0)

    for t, stages in ((n_tiles, "bc" "bbbbbbb"), (n_tiles + 1, "c")):
        step, slot = t, t % 2
        out_copy(t - 4, slot).wait()
        run(stages)
        out_copy(t - 2, slot).start()
    for t in (n_tiles - 2, n_tiles - 1):
        out_copy(t, t % 2).wait()


def _resident(a):
    del a
    return pl.BlockSpec(memory_space=pltpu.VMEM)


def kernel(x, norm1_g, w_in, conv_w, gmlp_ln_g, gmlp_ln_b, gmlp_ws, gmlp_bs, out_norm_conv_g,
           out_norm_gmlp_g, w_out, norm2_g, w_up, w_down, final_g):
    depth = w_in.shape[0]
    batch, seq, d_model = x.shape
    c = conv_w.shape[1]
    gw = gmlp_ln_g.shape[1]
    head_dim = gw // GMLP_HEADS
    assert depth == 1, "one fused block per call"
    assert seq % TILE == 0 and TILE % GMLP_BLOCK == 0
    assert head_dim * HEADS_PER_VREG == LANES
    assert w_up.shape[2] % N_FF_CHUNKS == 0
    tiles_per_batch = seq // TILE
    n_tiles = batch * tiles_per_batch
    bf16 = jnp.bfloat16
    row = lambda a: a.reshape(1, -1)
    bias = jnp.repeat(jnp.transpose(gmlp_bs[0]), head_dim, axis=1)

    in_hbm = pl.BlockSpec(memory_space=pl.ANY)
    operands_and_specs = (
        (x.reshape(batch * seq, d_model), in_hbm),
        (row(norm1_g[0]), None), (w_in[0], in_hbm), (jnp.transpose(conv_w[0]), None),
        (row(gmlp_ln_g[0]), None), (row(gmlp_ln_b[0]), None), (gmlp_ws[0], None), (bias, None),
        (row(out_norm_conv_g[0]), None), (row(out_norm_gmlp_g[0]), None), (w_out[0], in_hbm),
        (row(norm2_g[0]), None), (w_up[0], in_hbm), (w_down[0], in_hbm), (row(final_g), None),
    )
    operands = [a for a, _ in operands_and_specs]
    in_specs = [_resident(a) if spec is None else spec for a, spec in operands_and_specs]
    out = pl.pallas_call(
        functools.partial(_block_kernel, n_tiles=n_tiles, tiles_per_batch=tiles_per_batch),
        in_specs=in_specs,
        out_specs=pl.BlockSpec(memory_space=pl.ANY),
        out_shape=jax.ShapeDtypeStruct((batch * seq, d_model), x.dtype),
        scratch_shapes=[
            pltpu.VMEM(w_in.shape[1:], bf16),
            pltpu.VMEM(w_out.shape[1:], bf16),
            pltpu.VMEM(w_up.shape[1:], bf16),
            pltpu.VMEM(w_down.shape[1:], bf16),
            pltpu.VMEM((GMLP_HEADS // HEADS_PER_VREG, GMLP_BLOCK, HEADS_PER_VREG * GMLP_BLOCK), bf16),
            pltpu.VMEM((TILE + SUBLANES, c), jnp.float32),
            pltpu.VMEM((TILE, gw), jnp.float32),
            pltpu.VMEM((TILE, d_model), jnp.float32),
            pltpu.VMEM((TILE, d_model), bf16),
            pltpu.VMEM((TILE, LANES), jnp.float32),
            pltpu.VMEM((TILE, d_model), jnp.float32),
            pltpu.VMEM((TILE, LANES), jnp.float32),
            pltpu.VMEM((2, TILE, d_model), jnp.float32),
            pltpu.VMEM((2, TILE, d_model), jnp.float32),
            pltpu.SemaphoreType.DMA((2,)),
            pltpu.SemaphoreType.DMA((2,)),
        ],
        compiler_params=pltpu.CompilerParams(vmem_limit_bytes=VMEM_LIMIT_BYTES),
        name="hybrid_block",
    )(*operands)
    return out.reshape(batch, seq, d_model)
```

```python
import functools

import jax
import jax.numpy as jnp
from jax import lax
from jax.experimental import pallas as pl
from jax.experimental.pallas import tpu as pltpu

EPS = 1e-6
CHUNK = 64
GMLP_BLOCK = 128
GMLP_HEADS = 8
LANES = 128
SUBLANES = 8
HEADS_PER_VREG = 2

TILE = 512
N_FF_CHUNKS = 4
VMEM_LIMIT_BYTES = 60000 * 1024

WEIGHT_CHUNK_BYTES = 1024 * 1024
WEIGHT_LOAD_SLOTS = 8
OVERLAPPED_LOAD_SLOTS = 5
OVERLAPPED_CHUNKS_PER_GROUP = 2


def _inv_rms(x):
    return lax.rsqrt(jnp.mean(x * x, axis=-1, keepdims=True) + EPS)


def _rms(x, g):
    return x * _inv_rms(x) * g


class _WeightStream:
    def __init__(self, src_hbm, dst_ref, stage_ref, sem_ref):
        self.src_hbm, self.dst_ref, self.stage_ref, self.sem_ref = src_hbm, dst_ref, stage_ref, sem_ref
        self.slots, self.rows, _ = stage_ref.shape
        assert src_hbm.shape[0] % self.rows == 0
        self.n_chunks = src_hbm.shape[0] // self.rows
        self.done = 0

    @staticmethod
    def staging(src_hbm, slots):
        n_cols = src_hbm.shape[1]
        rows = 1 << ((WEIGHT_CHUNK_BYTES // (n_cols * 4)).bit_length() - 1)
        return (pltpu.VMEM((slots, rows, n_cols), jnp.float32), pltpu.SemaphoreType.DMA((slots,)))

    def _copy(self, i, slot):
        return pltpu.make_async_copy(self.src_hbm.at[pl.ds(i * self.rows, self.rows), :],
                                     self.stage_ref.at[slot], self.sem_ref.at[slot])

    def start(self):
        for i in range(min(self.slots - 1, self.n_chunks)):
            self._copy(i, i).start()

    def advance(self, count):
        first, last = self.done, min(self.done + count, self.n_chunks)

        def step(i, carry):
            ahead = i + self.slots - 1

            @pl.when(ahead < self.n_chunks)
            def _():
                self._copy(ahead, ahead % self.slots).start()

            slot = i % self.slots
            self._copy(i, slot).wait()
            self.dst_ref[pl.ds(pl.multiple_of(i * self.rows, self.rows), self.rows), :] = (
                self.stage_ref[slot].astype(jnp.bfloat16))
            return carry

        lax.fori_loop(first, last, step, 0)
        self.done = last

    def finish(self):
        self.advance(self.n_chunks - self.done)


def _load_weight_as_bf16(src_hbm, dst_ref):
    def body(stage_ref, sem_ref):
        stream = _WeightStream(src_hbm, dst_ref, stage_ref, sem_ref)
        stream.start()
        stream.finish()

    pl.run_scoped(body, *_WeightStream.staging(src_hbm, WEIGHT_LOAD_SLOTS))


def _block_kernel(x_hbm, g1_ref, w_in_hbm, conv_w_ref, ln_g_ref, ln_b_ref, ws_ref, bias_ref,
                  gc_ref, gg_ref, w_out_hbm, g2_ref, w_up_hbm, w_down_hbm, gf_ref,
                  out_hbm, w_in_ref, w_out_ref, w_up_ref, w_down_ref,
                  ws_pair_ref, zbuf_ref, yb_ref, x1_ref, h2_ref, rr2_ref, pre_ref, ssq_ref,
                  x_buf, o_buf, x_sem, o_sem, *, n_tiles, tiles_per_batch):
    f32, bf16 = jnp.float32, jnp.bfloat16
    tile = TILE
    c = conv_w_ref.shape[1]
    gw = ln_g_ref.shape[1]
    ff_chunk = w_up_ref.shape[1] // N_FF_CHUNKS
    n_blocks = tile // GMLP_BLOCK
    n_pairs = GMLP_HEADS // HEADS_PER_VREG
    step = slot = None

    def prepare():
        qi = lax.broadcasted_iota(jnp.int32, (GMLP_BLOCK, GMLP_BLOCK), 0) // CHUNK
        kj = lax.broadcasted_iota(jnp.int32, (GMLP_BLOCK, GMLP_BLOCK), 1) // CHUNK
        causal = kj <= qi
        for h in range(GMLP_HEADS):
            g, r = divmod(h, HEADS_PER_VREG)
            ws_pair_ref[g, :, r * GMLP_BLOCK:(r + 1) * GMLP_BLOCK] = (
                jnp.where(causal, ws_ref[h], 0.0).astype(bf16))
        zbuf_ref[0:SUBLANES, :] = jnp.zeros((SUBLANES, c), f32)
        pre_ref[...] = jnp.zeros(pre_ref.shape, f32)
        ssq_ref[...] = jnp.zeros(ssq_ref.shape, f32)

    def stage_a():
        x = x_buf[slot]
        h = _rms(x, g1_ref[...]).astype(bf16)
        yield

        def in_proj(lo, width):
            return jnp.dot(h, w_in_ref[:, lo:lo + width], preferred_element_type=f32)

        z = in_proj(c, c) * in_proj(2 * c, c)
        zbuf_ref[SUBLANES:SUBLANES + tile, :] = z
        cw = conv_w_ref[...]
        conv = (zbuf_ref[SUBLANES - 2:SUBLANES - 2 + tile, :] * cw[0:1, :]
                + zbuf_ref[SUBLANES - 1:SUBLANES - 1 + tile, :] * cw[1:2, :]
                + z * cw[2:3, :])
        next_starts_sequence = (step + 1) % tiles_per_batch == 0
        zbuf_ref[0:SUBLANES, :] = jnp.where(next_starts_sequence, 0.0,
                                            zbuf_ref[tile:tile + SUBLANES, :])
        yield
        y_a = _rms(in_proj(0, c) * conv, gc_ref[...]).astype(bf16)
        yield

        v = in_proj(3 * c + gw, gw)
        mu = jnp.mean(v, axis=-1, keepdims=True)
        vc = v - mu
        vn = vc * lax.rsqrt(jnp.mean(vc * vc, axis=-1, keepdims=True) + EPS)
        vn = vn * ln_g_ref[...] + ln_b_ref[...]
        yield
        lane = lax.broadcasted_iota(jnp.int32, (tile, LANES), 1)
        low_head = lane < (LANES // HEADS_PER_VREG)
        for g in range(n_pairs):
            vg = vn[:, g * LANES:(g + 1) * LANES]
            v_lo = jnp.where(low_head, vg, 0.0).astype(bf16)
            v_hi = jnp.where(low_head, 0.0, vg).astype(bf16)
            rhs = jnp.concatenate(
                [jnp.concatenate([v_lo[n * GMLP_BLOCK:(n + 1) * GMLP_BLOCK, :]
                                  for n in range(n_blocks)], axis=1),
                 jnp.concatenate([v_hi[n * GMLP_BLOCK:(n + 1) * GMLP_BLOCK, :]
                                  for n in range(n_blocks)], axis=1)],
                axis=0)
            mixed = jnp.dot(ws_pair_ref[g], rhs, preferred_element_type=f32)
            for n in range(n_blocks):
                yb_ref[n * GMLP_BLOCK:(n + 1) * GMLP_BLOCK, g * LANES:(g + 1) * LANES] = (
                    mixed[:, n * LANES:(n + 1) * LANES])
        yield
        u = in_proj(3 * c, gw)
        bias = jnp.concatenate([bias_ref[...]] * n_blocks, axis=0)
        y_b = _rms(u * (yb_ref[...] + bias), gg_ref[...]).astype(bf16)
        yield

        y = jnp.concatenate([y_a, y_b], axis=-1)
        x1 = x + jnp.dot(y, w_out_ref[...], preferred_element_type=f32)
        x1_ref[...] = x1
        h2_ref[...] = (x1 * g2_ref[...]).astype(bf16)
        r2 = _inv_rms(x1)
        rr2_ref[...] = jnp.broadcast_to(r2 * r2, rr2_ref.shape)
        yield

    def stage_b():
        rr2 = jnp.concatenate([rr2_ref[...]] * (ff_chunk // LANES), axis=1)
        acc = None
        for k in range(N_FF_CHUNKS):
            cols = slice(k * ff_chunk, (k + 1) * ff_chunk)
            a = jnp.maximum(
                jnp.dot(h2_ref[...], w_up_ref[:, cols], preferred_element_type=f32), 0.0)
            a = (a * a * rr2).astype(bf16)
            yield
            if acc is None:
                acc = x1_ref[...]
            acc = acc + jnp.dot(a, w_down_ref[cols, :], preferred_element_type=f32)
            if k == N_FF_CHUNKS - 1:
                pre_ref[...] = acc
                sq = acc * acc
                ssq_ref[...] = sum(sq[:, j:j + LANES] for j in range(0, sq.shape[1], LANES))
            yield

    def stage_c():
        pre = pre_ref[...]
        ms = jnp.sum(ssq_ref[...], axis=-1, keepdims=True) * (1.0 / pre.shape[1])
        o_buf[slot] = pre * lax.rsqrt(ms + EPS) * gf_ref[...]

    def run(order, after_group=None):
        b_groups = [i for i, name in enumerate(order) if name == "b"]
        if "a" in order and b_groups:
            assert order.rindex("a") > b_groups[0::2][-1]
        if "c" in order and b_groups:
            assert order.index("c") < b_groups[-1]
        gens = {}
        for name in order:
            if name == "c":
                stage_c()
                continue
            if name not in gens:
                gens[name] = {"a": stage_a, "b": stage_b}[name]()
            next(gens[name])
            if after_group is not None:
                after_group()
        for gen in gens.values():
            assert next(gen, "done") == "done"

    def first_step():
        _load_weight_as_bf16(w_in_hbm, w_in_ref)
        _load_weight_as_bf16(w_out_hbm, w_out_ref)
        prepare()

        def mix_while_loading(up_stage, up_sem, down_stage, down_sem):
            streams = (_WeightStream(w_up_hbm, w_up_ref, up_stage, up_sem),
                       _WeightStream(w_down_hbm, w_down_ref, down_stage, down_sem))
            for stream in streams:
                stream.start()

            def drain_some():
                for stream in streams:
                    stream.advance(OVERLAPPED_CHUNKS_PER_GROUP)

            run("aaaaaaa", after_group=drain_some)
            for stream in streams:
                stream.finish()

        pl.run_scoped(mix_while_loading,
                      *_WeightStream.staging(w_up_hbm, OVERLAPPED_LOAD_SLOTS),
                      *_WeightStream.staging(w_down_hbm, OVERLAPPED_LOAD_SLOTS))

    def x_copy(t, buf):
        return pltpu.make_async_copy(x_hbm.at[pl.ds(t * tile, tile), :], x_buf.at[buf], x_sem.at[buf])

    def out_copy(t, buf):
        return pltpu.make_async_copy(o_buf.at[buf], out_hbm.at[pl.ds(t * tile, tile), :], o_sem.at[buf])

    assert n_tiles >= 4

    step, slot = 0, 0
    x_copy(0, 0).start()
    x_copy(1, 1).start()
    x_copy(0, 0).wait()
    first_step()
    x_copy(2, 0).start()

    def steady_pass(t, carry):
        nonlocal step, slot
        step, slot = t, t % 2
        x_copy(t, slot).wait()
        pl.when(t >= 4)(lambda: out_copy(t - 4, slot).wait())
        run("bc" "ab" "aab" "ab" "b" "aa" "bb" "ab")
        pl.when(t >= 2)(lambda: out_copy(t - 2, slot).start())
        pl.when(t + 2 < n_tiles)(lambda: x_copy(t + 2, slot).start())
        return carry

    lax.fori_loop(1, n_tiles, steady_pass, 0)

    for t, stages in ((n_tiles, "bc" "bbbbbbb"), (n_tiles + 1, "c")):
        step, slot = t, t % 2
        out_copy(t - 4, slot).wait()
        run(stages)
        out_copy(t - 2, slot).start()
    for t in (n_tiles - 2, n_tiles - 1):
        out_copy(t, t % 2).wait()


def _resident(a):
    del a
    return pl.BlockSpec(memory_space=pltpu.VMEM)


def kernel(x, norm1_g, w_in, conv_w, gmlp_ln_g, gmlp_ln_b, gmlp_ws, gmlp_bs, out_norm_conv_g,
           out_norm_gmlp_g, w_out, norm2_g, w_up, w_down, final_g):
    depth = w_in.shape[0]
    batch, seq, d_model = x.shape
    c = conv_w.shape[1]
    gw = gmlp_ln_g.shape[1]
    head_dim = gw // GMLP_HEADS
    assert depth == 1, "one fused block per call"
    assert seq % TILE == 0 and TILE % GMLP_BLOCK == 0
    assert head_dim * HEADS_PER_VREG == LANES
    assert w_up.shape[2] % N_FF_CHUNKS == 0
    tiles_per_batch = seq // TILE
    n_tiles = batch * tiles_per_batch
    bf16 = jnp.bfloat16
    row = lambda a: a.reshape(1, -1)
    bias = jnp.repeat(jnp.transpose(gmlp_bs[0]), head_dim, axis=1)

    in_hbm = pl.BlockSpec(memory_space=pl.ANY)
    operands_and_specs = (
        (x.reshape(batch * seq, d_model), in_hbm),
        (row(norm1_g[0]), None), (w_in[0], in_hbm), (jnp.transpose(conv_w[0]), None),
        (row(gmlp_ln_g[0]), None), (row(gmlp_ln_b[0]), None), (gmlp_ws[0], None), (bias, None),
        (row(out_norm_conv_g[0]), None), (row(out_norm_gmlp_g[0]), None), (w_out[0], in_hbm),
        (row(norm2_g[0]), None), (w_up[0], in_hbm), (w_down[0], in_hbm), (row(final_g), None),
    )
    operands = [a for a, _ in operands_and_specs]
    in_specs = [_resident(a) if spec is None else spec for a, spec in operands_and_specs]
    out = pl.pallas_call(
        functools.partial(_block_kernel, n_tiles=n_tiles, tiles_per_batch=tiles_per_batch),
        in_specs=in_specs,
        out_specs=pl.BlockSpec(memory_space=pl.ANY),
        out_shape=jax.ShapeDtypeStruct((batch * seq, d_model), x.dtype),
        scratch_shapes=[
            pltpu.VMEM(w_in.shape[1:], bf16),
            pltpu.VMEM(w_out.shape[1:], bf16),
            pltpu.VMEM(w_up.shape[1:], bf16),
            pltpu.VMEM(w_down.shape[1:], bf16),
            pltpu.VMEM((GMLP_HEADS // HEADS_PER_VREG, GMLP_BLOCK, HEADS_PER_VREG * GMLP_BLOCK), bf16),
            pltpu.VMEM((TILE + SUBLANES, c), jnp.float32),
            pltpu.VMEM((TILE, gw), jnp.float32),
            pltpu.VMEM((TILE, d_model), jnp.float32),
            pltpu.VMEM((TILE, d_model), bf16),
            pltpu.VMEM((TILE, LANES), jnp.float32),
            pltpu.VMEM((TILE, d_model), jnp.float32),
            pltpu.VMEM((TILE, LANES), jnp.float32),
            pltpu.VMEM((2, TILE, d_model), jnp.float32),
            pltpu.VMEM((2, TILE, d_model), jnp.float32),
            pltpu.SemaphoreType.DMA((2,)),
            pltpu.SemaphoreType.DMA((2,)),
        ],
        compiler_params=pltpu.CompilerParams(vmem_limit_bytes=VMEM_LIMIT_BYTES),
        name="hybrid_block",
    )(*operands)
    return out.reshape(batch, seq, d_model)
```

```python
import functools

import jax
import jax.numpy as jnp
from jax import lax
from jax.experimental import pallas as pl
from jax.experimental.pallas import tpu as pltpu

EPS = 1e-6
CHUNK = 64
GMLP_BLOCK = 128
GMLP_HEADS = 8
LANES = 128
SUBLANES = 8
HEADS_PER_VREG = 2

TILE = 512
N_FF_CHUNKS = 4
VMEM_LIMIT_BYTES = 60000 * 1024

WEIGHT_CHUNK_BYTES = 1024 * 1024
WEIGHT_LOAD_SLOTS = 8
OVERLAPPED_LOAD_SLOTS = 5
OVERLAPPED_CHUNKS_PER_GROUP = 2


def _inv_rms(x):
    return lax.rsqrt(jnp.mean(x * x, axis=-1, keepdims=True) + EPS)


def _rms(x, g):
    return x * _inv_rms(x) * g


class _WeightStream:
    def __init__(self, src_hbm, dst_ref, stage_ref, sem_ref):
        self.src_hbm, self.dst_ref, self.stage_ref, self.sem_ref = src_hbm, dst_ref, stage_ref, sem_ref
        self.slots, self.rows, _ = stage_ref.shape
        assert src_hbm.shape[0] % self.rows == 0
        self.n_chunks = src_hbm.shape[0] // self.rows
        self.done = 0

    @staticmethod
    def staging(src_hbm, slots):
        n_cols = src_hbm.shape[1]
        rows = 1 << ((WEIGHT_CHUNK_BYTES // (n_cols * 4)).bit_length() - 1)
        return (pltpu.VMEM((slots, rows, n_cols), jnp.float32), pltpu.SemaphoreType.DMA((slots,)))

    def _copy(self, i, slot):
        return pltpu.make_async_copy(self.src_hbm.at[pl.ds(i * self.rows, self.rows), :],
                                     self.stage_ref.at[slot], self.sem_ref.at[slot])

    def start(self):
        for i in range(min(self.slots - 1, self.n_chunks)):
            self._copy(i, i).start()

    def advance(self, count):
        first, last = self.done, min(self.done + count, self.n_chunks)

        def step(i, carry):
            ahead = i + self.slots - 1

            @pl.when(ahead < self.n_chunks)
            def _():
                self._copy(ahead, ahead % self.slots).start()

            slot = i % self.slots
            self._copy(i, slot).wait()
            self.dst_ref[pl.ds(pl.multiple_of(i * self.rows, self.rows), self.rows), :] = (
                self.stage_ref[slot].astype(jnp.bfloat16))
            return carry

        lax.fori_loop(first, last, step, 0)
        self.done = last

    def finish(self):
        self.advance(self.n_chunks - self.done)


def _load_weight_as_bf16(src_hbm, dst_ref):
    def body(stage_ref, sem_ref):
        stream = _WeightStream(src_hbm, dst_ref, stage_ref, sem_ref)
        stream.start()
        stream.finish()

    pl.run_scoped(body, *_WeightStream.staging(src_hbm, WEIGHT_LOAD_SLOTS))


def _block_kernel(x_hbm, g1_ref, w_in_hbm, conv_w_ref, ln_g_ref, ln_b_ref, ws_ref, bias_ref,
                  gc_ref, gg_ref, w_out_hbm, g2_ref, w_up_hbm, w_down_hbm, gf_ref,
                  out_hbm, w_in_ref, w_out_ref, w_up_ref, w_down_ref,
                  ws_pair_ref, zbuf_ref, yb_ref, x1_ref, h2_ref, rr2_ref, pre_ref, ssq_ref,
                  x_buf, o_buf, x_sem, o_sem, *, n_tiles, tiles_per_batch):
    f32, bf16 = jnp.float32, jnp.bfloat16
    tile = TILE
    c = conv_w_ref.shape[1]
    gw = ln_g_ref.shape[1]
    ff_chunk = w_up_ref.shape[1] // N_FF_CHUNKS
    n_blocks = tile // GMLP_BLOCK
    n_pairs = GMLP_HEADS // HEADS_PER_VREG
    step = slot = None

    def prepare():
        qi = lax.broadcasted_iota(jnp.int32, (GMLP_BLOCK, GMLP_BLOCK), 0) // CHUNK
        kj = lax.broadcasted_iota(jnp.int32, (GMLP_BLOCK, GMLP_BLOCK), 1) // CHUNK
        causal = kj <= qi
        for h in range(GMLP_HEADS):
            g, r = divmod(h, HEADS_PER_VREG)
            ws_pair_ref[g, :, r * GMLP_BLOCK:(r + 1) * GMLP_BLOCK] = (
                jnp.where(causal, ws_ref[h], 0.0).astype(bf16))
        zbuf_ref[0:SUBLANES, :] = jnp.zeros((SUBLANES, c), f32)
        pre_ref[...] = jnp.zeros(pre_ref.shape, f32)
        ssq_ref[...] = jnp.zeros(ssq_ref.shape, f32)

    def stage_a():
        x = x_buf[slot]
        h = _rms(x, g1_ref[...]).astype(bf16)
        yield

        def in_proj(lo, width):
            return jnp.dot(h, w_in_ref[:, lo:lo + width], preferred_element_type=f32)

        z = in_proj(c, c) * in_proj(2 * c, c)
        zbuf_ref[SUBLANES:SUBLANES + tile, :] = z
        cw = conv_w_ref[...]
        conv = (zbuf_ref[SUBLANES - 2:SUBLANES - 2 + tile, :] * cw[0:1, :]
                + zbuf_ref[SUBLANES - 1:SUBLANES - 1 + tile, :] * cw[1:2, :]
                + z * cw[2:3, :])
        next_starts_sequence = (step + 1) % tiles_per_batch == 0
        zbuf_ref[0:SUBLANES, :] = jnp.where(next_starts_sequence, 0.0,
                                            zbuf_ref[tile:tile + SUBLANES, :])
        yield
        y_a = _rms(in_proj(0, c) * conv, gc_ref[...]).astype(bf16)
        yield

        v = in_proj(3 * c + gw, gw)
        mu = jnp.mean(v, axis=-1, keepdims=True)
        vc = v - mu
        vn = vc * lax.rsqrt(jnp.mean(vc * vc, axis=-1, keepdims=True) + EPS)
        vn = vn * ln_g_ref[...] + ln_b_ref[...]
        yield
        lane = lax.broadcasted_iota(jnp.int32, (tile, LANES), 1)
        low_head = lane < (LANES // HEADS_PER_VREG)
        for g in range(n_pairs):
            vg = vn[:, g * LANES:(g + 1) * LANES]
            v_lo = jnp.where(low_head, vg, 0.0).astype(bf16)
            v_hi = jnp.where(low_head, 0.0, vg).astype(bf16)
            rhs = jnp.concatenate(
                [jnp.concatenate([v_lo[n * GMLP_BLOCK:(n + 1) * GMLP_BLOCK, :]
                                  for n in range(n_blocks)], axis=1),
                 jnp.concatenate([v_hi[n * GMLP_BLOCK:(n + 1) * GMLP_BLOCK, :]
                                  for n in range(n_blocks)], axis=1)],
                axis=0)
            mixed = jnp.dot(ws_pair_ref[g], rhs, preferred_element_type=f32)
            for n in range(n_blocks):
                yb_ref[n * GMLP_BLOCK:(n + 1) * GMLP_BLOCK, g * LANES:(g + 1) * LANES] = (
                    mixed[:, n * LANES:(n + 1) * LANES])
        yield
        u = in_proj(3 * c, gw)
        bias = jnp.concatenate([bias_ref[...]] * n_blocks, axis=0)
        y_b = _rms(u * (yb_ref[...] + bias), gg_ref[...]).astype(bf16)
        yield

        y = jnp.concatenate([y_a, y_b], axis=-1)
        x1 = x_buf[slot] + jnp.dot(y, w_out_ref[...], preferred_element_type=f32)
        x1_ref[...] = x1
        h2_ref[...] = (x1 * g2_ref[...]).astype(bf16)
        r2 = _inv_rms(x1)
        rr2_ref[...] = jnp.broadcast_to(r2 * r2, rr2_ref.shape)
        yield

    def stage_b():
        acc = None
        for k in range(N_FF_CHUNKS):
            cols = slice(k * ff_chunk, (k + 1) * ff_chunk)
            a = jnp.maximum(
                jnp.dot(h2_ref[...], w_up_ref[:, cols], preferred_element_type=f32), 0.0)
            rr2 = jnp.concatenate([rr2_ref[...]] * (ff_chunk // LANES), axis=1)
            a = (a * a * rr2).astype(bf16)
            yield
            if acc is None:
                acc = x1_ref[...]
            acc = acc + jnp.dot(a, w_down_ref[cols, :], preferred_element_type=f32)
            if k == N_FF_CHUNKS - 1:
                pre_ref[...] = acc
                sq = acc * acc
                ssq_ref[...] = sum(sq[:, j:j + LANES] for j in range(0, sq.shape[1], LANES))
            yield

    def stage_c():
        pre = pre_ref[...]
        ms = jnp.sum(ssq_ref[...], axis=-1, keepdims=True) * (1.0 / pre.shape[1])
        o_buf[slot] = pre * lax.rsqrt(ms + EPS) * gf_ref[...]

    def run(order, after_group=None):
        b_groups = [i for i, name in enumerate(order) if name == "b"]
        if "a" in order and b_groups:
            assert order.rindex("a") > b_groups[0::2][-1]
        if "c" in order and b_groups:
            assert order.index("c") < b_groups[-1]
        gens = {}
        for name in order:
            if name == "c":
                stage_c()
                continue
            if name not in gens:
                gens[name] = {"a": stage_a, "b": stage_b}[name]()
            next(gens[name])
            if after_group is not None:
                after_group()
        for gen in gens.values():
            assert next(gen, "done") == "done"

    def first_step():
        _load_weight_as_bf16(w_in_hbm, w_in_ref)
        _load_weight_as_bf16(w_out_hbm, w_out_ref)
        prepare()

        def mix_while_loading(up_stage, up_sem, down_stage, down_sem):
            streams = (_WeightStream(w_up_hbm, w_up_ref, up_stage, up_sem),
                       _WeightStream(w_down_hbm, w_down_ref, down_stage, down_sem))
            for stream in streams:
                stream.start()

            def drain_some():
                for stream in streams:
                    stream.advance(OVERLAPPED_CHUNKS_PER_GROUP)

            run("aaaaaaa", after_group=drain_some)
            for stream in streams:
                stream.finish()

        pl.run_scoped(mix_while_loading,
                      *_WeightStream.staging(w_up_hbm, OVERLAPPED_LOAD_SLOTS),
                      *_WeightStream.staging(w_down_hbm, OVERLAPPED_LOAD_SLOTS))

    def x_copy(t, buf):
        return pltpu.make_async_copy(x_hbm.at[pl.ds(t * tile, tile), :], x_buf.at[buf], x_sem.at[buf])

    def out_copy(t, buf):
        return pltpu.make_async_copy(o_buf.at[buf], out_hbm.at[pl.ds(t * tile, tile), :], o_sem.at[buf])

    assert n_tiles >= 4

    step, slot = 0, 0
    x_copy(0, 0).start()
    x_copy(1, 1).start()
    x_copy(0, 0).wait()
    first_step()

    def steady_pass(t, carry):
        nonlocal step, slot
        step, slot = t, t % 2
        pl.when(t + 1 < n_tiles)(lambda: x_copy(t + 1, 1 - slot).start())
        x_copy(t, slot).wait()
        pl.when(t >= 4)(lambda: out_copy(t - 4, slot).wait())
        run("bc" "ab" "aab" "ab" "b" "aa" "bb" "ab")
        pl.when(t >= 2)(lambda: out_copy(t - 2, slot).start())
        return carry

    lax.fori_loop(1, n_tiles, steady_pass, 0)

    for t, stages in ((n_tiles, "bc" "bbbbbbb"), (n_tiles + 1, "c")):
        step, slot = t, t % 2
        out_copy(t - 4, slot).wait()
        run(stages)
        out_copy(t - 2, slot).start()
    for t in (n_tiles - 2, n_tiles - 1):
        out_copy(t, t % 2).wait()


def _resident(a):
    del a
    return pl.BlockSpec(memory_space=pltpu.VMEM)


def kernel(x, norm1_g, w_in, conv_w, gmlp_ln_g, gmlp_ln_b, gmlp_ws, gmlp_bs, out_norm_conv_g,
           out_norm_gmlp_g, w_out, norm2_g, w_up, w_down, final_g):
    depth = w_in.shape[0]
    batch, seq, d_model = x.shape
    c = conv_w.shape[1]
    gw = gmlp_ln_g.shape[1]
    head_dim = gw // GMLP_HEADS
    assert depth == 1, "one fused block per call"
    assert seq % TILE == 0 and TILE % GMLP_BLOCK == 0
    assert head_dim * HEADS_PER_VREG == LANES
    assert w_up.shape[2] % N_FF_CHUNKS == 0
    tiles_per_batch = seq // TILE
    n_tiles = batch * tiles_per_batch
    bf16 = jnp.bfloat16
    row = lambda a: a.reshape(1, -1)
    bias = jnp.repeat(jnp.transpose(gmlp_bs[0]), head_dim, axis=1)

    in_hbm = pl.BlockSpec(memory_space=pl.ANY)
    operands_and_specs = (
        (x.reshape(batch * seq, d_model), in_hbm),
        (row(norm1_g[0]), None), (w_in[0], in_hbm), (jnp.transpose(conv_w[0]), None),
        (row(gmlp_ln_g[0]), None), (row(gmlp_ln_b[0]), None), (gmlp_ws[0], None), (bias, None),
        (row(out_norm_conv_g[0]), None), (row(out_norm_gmlp_g[0]), None), (w_out[0], in_hbm),
        (row(norm2_g[0]), None), (w_up[0], in_hbm), (w_down[0], in_hbm), (row(final_g), None),
    )
    operands = [a for a, _ in operands_and_specs]
    in_specs = [_resident(a) if spec is None else spec for a, spec in operands_and_specs]
    out = pl.pallas_call(
        functools.partial(_block_kernel, n_tiles=n_tiles, tiles_per_batch=tiles_per_batch),
        in_specs=in_specs,
        out_specs=pl.BlockSpec(memory_space=pl.ANY),
        out_shape=jax.ShapeDtypeStruct((batch * seq, d_model), x.dtype),
        scratch_shapes=[
            pltpu.VMEM(w_in.shape[1:], bf16),
            pltpu.VMEM(w_out.shape[1:], bf16),
            pltpu.VMEM(w_up.shape[1:], bf16),
            pltpu.VMEM(w_down.shape[1:], bf16),
            pltpu.VMEM((GMLP_HEADS // HEADS_PER_VREG, GMLP_BLOCK, HEADS_PER_VREG * GMLP_BLOCK), bf16),
            pltpu.VMEM((TILE + SUBLANES, c), jnp.float32),
            pltpu.VMEM((TILE, gw), jnp.float32),
            pltpu.VMEM((TILE, d_model), jnp.float32),
            pltpu.VMEM((TILE, d_model), bf16),
            pltpu.VMEM((TILE, LANES), jnp.float32),
            pltpu.VMEM((TILE, d_model), jnp.float32),
            pltpu.VMEM((TILE, LANES), jnp.float32),
            pltpu.VMEM((2, TILE, d_model), jnp.float32),
            pltpu.VMEM((2, TILE, d_model), jnp.float32),
            pltpu.SemaphoreType.DMA((2,)),
            pltpu.SemaphoreType.DMA((2,)),
        ],
        compiler_params=pltpu.CompilerParams(vmem_limit_bytes=VMEM_LIMIT_BYTES),
        name="hybrid_block",
    )(*operands)
    return out.reshape(batch, seq, d_model)
```

```python
import functools

import jax
import jax.numpy as jnp
from jax import lax
from jax.experimental import pallas as pl
from jax.experimental.pallas import tpu as pltpu

EPS = 1e-6
CHUNK = 64
GMLP_BLOCK = 128
GMLP_HEADS = 8
LANES = 128
SUBLANES = 8
HEADS_PER_VREG = 2

TILE = 512
N_FF_CHUNKS = 4
VMEM_LIMIT_BYTES = 60000 * 1024

WEIGHT_CHUNK_BYTES = 1024 * 1024
WEIGHT_LOAD_SLOTS = 8
OVERLAPPED_LOAD_SLOTS = 5
OVERLAPPED_CHUNKS_PER_GROUP = 2


def _inv_rms(x):
    return lax.rsqrt(jnp.mean(x * x, axis=-1, keepdims=True) + EPS)


def _rms(x, g):
    return x * _inv_rms(x) * g


class _WeightStream:
    def __init__(self, src_hbm, dst_ref, stage_ref, sem_ref):
        self.src_hbm, self.dst_ref, self.stage_ref, self.sem_ref = src_hbm, dst_ref, stage_ref, sem_ref
        self.slots, self.rows, _ = stage_ref.shape
        assert src_hbm.shape[0] % self.rows == 0
        self.n_chunks = src_hbm.shape[0] // self.rows
        self.done = 0

    @staticmethod
    def staging(src_hbm, slots):
        n_cols = src_hbm.shape[1]
        rows = 1 << ((WEIGHT_CHUNK_BYTES // (n_cols * 4)).bit_length() - 1)
        return (pltpu.VMEM((slots, rows, n_cols), jnp.float32), pltpu.SemaphoreType.DMA((slots,)))

    def _copy(self, i, slot):
        return pltpu.make_async_copy(self.src_hbm.at[pl.ds(i * self.rows, self.rows), :],
                                     self.stage_ref.at[slot], self.sem_ref.at[slot])

    def start(self):
        for i in range(min(self.slots - 1, self.n_chunks)):
            self._copy(i, i).start()

    def advance(self, count):
        first, last = self.done, min(self.done + count, self.n_chunks)

        def step(i, carry):
            ahead = i + self.slots - 1

            @pl.when(ahead < self.n_chunks)
            def _():
                self._copy(ahead, ahead % self.slots).start()

            slot = i % self.slots
            self._copy(i, slot).wait()
            self.dst_ref[pl.ds(pl.multiple_of(i * self.rows, self.rows), self.rows), :] = (
                self.stage_ref[slot].astype(jnp.bfloat16))
            return carry

        lax.fori_loop(first, last, step, 0)
        self.done = last

    def finish(self):
        self.advance(self.n_chunks - self.done)


def _load_weight_as_bf16(src_hbm, dst_ref):
    def body(stage_ref, sem_ref):
        stream = _WeightStream(src_hbm, dst_ref, stage_ref, sem_ref)
        stream.start()
        stream.finish()

    pl.run_scoped(body, *_WeightStream.staging(src_hbm, WEIGHT_LOAD_SLOTS))


def _block_kernel(x_hbm, g1_ref, w_in_hbm, conv_w_ref, ln_g_ref, ln_b_ref, ws_ref, bias_ref,
                  gc_ref, gg_ref, w_out_hbm, g2_ref, w_up_hbm, w_down_hbm, gf_ref,
                  out_hbm, w_in_ref, w_out_ref, w_up_ref, w_down_ref,
                  ws_pair_ref, zbuf_ref, yb_ref, x1_ref, h2_ref, rr2_ref, pre_ref, ssq_ref,
                  x_buf, o_buf, x_sem, o_sem, *, n_tiles, tiles_per_batch):
    f32, bf16 = jnp.float32, jnp.bfloat16
    tile = TILE
    c = conv_w_ref.shape[1]
    gw = ln_g_ref.shape[1]
    ff_chunk = w_up_ref.shape[1] // N_FF_CHUNKS
    n_blocks = tile // GMLP_BLOCK
    n_pairs = GMLP_HEADS // HEADS_PER_VREG
    step = slot = None

    def prepare():
        qi = lax.broadcasted_iota(jnp.int32, (GMLP_BLOCK, GMLP_BLOCK), 0) // CHUNK
        kj = lax.broadcasted_iota(jnp.int32, (GMLP_BLOCK, GMLP_BLOCK), 1) // CHUNK
        causal = kj <= qi
        for h in range(GMLP_HEADS):
            g, r = divmod(h, HEADS_PER_VREG)
            ws_pair_ref[g, :, r * GMLP_BLOCK:(r + 1) * GMLP_BLOCK] = (
                jnp.where(causal, ws_ref[h], 0.0).astype(bf16))
        zbuf_ref[0:SUBLANES, :] = jnp.zeros((SUBLANES, c), f32)
        pre_ref[...] = jnp.zeros(pre_ref.shape, f32)
        ssq_ref[...] = jnp.zeros(ssq_ref.shape, f32)

    def stage_a():
        x = x_buf[slot]
        h = _rms(x, g1_ref[...]).astype(bf16)
        yield

        def in_proj(lo, width):
            return jnp.dot(h, w_in_ref[:, lo:lo + width], preferred_element_type=f32)

        z = in_proj(c, c) * in_proj(2 * c, c)
        zbuf_ref[SUBLANES:SUBLANES + tile, :] = z
        cw = conv_w_ref[...]
        conv = (zbuf_ref[SUBLANES - 2:SUBLANES - 2 + tile, :] * cw[0:1, :]
                + zbuf_ref[SUBLANES - 1:SUBLANES - 1 + tile, :] * cw[1:2, :]
                + z * cw[2:3, :])
        next_starts_sequence = (step + 1) % tiles_per_batch == 0
        zbuf_ref[0:SUBLANES, :] = jnp.where(next_starts_sequence, 0.0,
                                            zbuf_ref[tile:tile + SUBLANES, :])
        yield
        y_a = _rms(in_proj(0, c) * conv, gc_ref[...]).astype(bf16)
        yield

        v = in_proj(3 * c + gw, gw)
        mu = jnp.mean(v, axis=-1, keepdims=True)
        vc = v - mu
        vn = vc * lax.rsqrt(jnp.mean(vc * vc, axis=-1, keepdims=True) + EPS)
        vn = vn * ln_g_ref[...] + ln_b_ref[...]
        yield
        lane = lax.broadcasted_iota(jnp.int32, (tile, LANES), 1)
        low_head = lane < (LANES // HEADS_PER_VREG)
        for g in range(n_pairs):
            vg = vn[:, g * LANES:(g + 1) * LANES]
            v_lo = jnp.where(low_head, vg, 0.0).astype(bf16)
            v_hi = jnp.where(low_head, 0.0, vg).astype(bf16)
            rhs = jnp.concatenate(
                [jnp.concatenate([v_lo[n * GMLP_BLOCK:(n + 1) * GMLP_BLOCK, :]
                                  for n in range(n_blocks)], axis=1),
                 jnp.concatenate([v_hi[n * GMLP_BLOCK:(n + 1) * GMLP_BLOCK, :]
                                  for n in range(n_blocks)], axis=1)],
                axis=0)
            mixed = jnp.dot(ws_pair_ref[g], rhs, preferred_element_type=f32)
            for n in range(n_blocks):
                yb_ref[n * GMLP_BLOCK:(n + 1) * GMLP_BLOCK, g * LANES:(g + 1) * LANES] = (
                    mixed[:, n * LANES:(n + 1) * LANES])
        yield
        u = in_proj(3 * c, gw)
        bias = jnp.concatenate([bias_ref[...]] * n_blocks, axis=0)
        y_b = _rms(u * (yb_ref[...] + bias), gg_ref[...]).astype(bf16)
        yield

        y = jnp.concatenate([y_a, y_b], axis=-1)
        x1 = x + jnp.dot(y, w_out_ref[...], preferred_element_type=f32)
        x1_ref[...] = x1
        h2_ref[...] = (x1 * g2_ref[...]).astype(bf16)
        r2 = _inv_rms(x1)
        rr2_ref[...] = jnp.broadcast_to(r2 * r2, rr2_ref.shape)
        yield

    def stage_b():
        rr2 = jnp.concatenate([rr2_ref[...]] * (ff_chunk // LANES), axis=1)
        acc = None
        for k in range(N_FF_CHUNKS):
            cols = slice(k * ff_chunk, (k + 1) * ff_chunk)
            up = jnp.dot(h2_ref[...], w_up_ref[:, cols], preferred_element_type=f32)
            halves = []
            for lo in (0, ff_chunk // 2):
                part = jnp.maximum(up[:, lo:lo + ff_chunk // 2], 0.0)
                halves.append((part * part * rr2[:, :ff_chunk // 2]).astype(bf16))
            a = jnp.concatenate(halves, axis=1)
            yield
            if acc is None:
                acc = x1_ref[...]
            acc = acc + jnp.dot(a, w_down_ref[cols, :], preferred_element_type=f32)
            if k == N_FF_CHUNKS - 1:
                pre_ref[...] = acc
                sq = acc * acc
                ssq_ref[...] = sum(sq[:, j:j + LANES] for j in range(0, sq.shape[1], LANES))
            yield

    def stage_c():
        pre = pre_ref[...]
        ms = jnp.sum(ssq_ref[...], axis=-1, keepdims=True) * (1.0 / pre.shape[1])
        o_buf[slot] = pre * lax.rsqrt(ms + EPS) * gf_ref[...]

    def run(order, after_group=None):
        b_groups = [i for i, name in enumerate(order) if name == "b"]
        if "a" in order and b_groups:
            assert order.rindex("a") > b_groups[0::2][-1]
        if "c" in order and b_groups:
            assert order.index("c") < b_groups[-1]
        gens = {}
        for name in order:
            if name == "c":
                stage_c()
                continue
            if name not in gens:
                gens[name] = {"a": stage_a, "b": stage_b}[name]()
            next(gens[name])
            if after_group is not None:
                after_group()
        for gen in gens.values():
            assert next(gen, "done") == "done"

    def first_step():
        _load_weight_as_bf16(w_in_hbm, w_in_ref)
        _load_weight_as_bf16(w_out_hbm, w_out_ref)
        prepare()

        def mix_while_loading(up_stage, up_sem, down_stage, down_sem):
            streams = (_WeightStream(w_up_hbm, w_up_ref, up_stage, up_sem),
                       _WeightStream(w_down_hbm, w_down_ref, down_stage, down_sem))
            for stream in streams:
                stream.start()

            def drain_some():
                for stream in streams:
                    stream.advance(OVERLAPPED_CHUNKS_PER_GROUP)

            run("aaaaaaa", after_group=drain_some)
            for stream in streams:
                stream.finish()

        pl.run_scoped(mix_while_loading,
                      *_WeightStream.staging(w_up_hbm, OVERLAPPED_LOAD_SLOTS),
                      *_WeightStream.staging(w_down_hbm, OVERLAPPED_LOAD_SLOTS))

    def x_copy(t, buf):
        return pltpu.make_async_copy(x_hbm.at[pl.ds(t * tile, tile), :], x_buf.at[buf], x_sem.at[buf])

    def out_copy(t, buf):
        return pltpu.make_async_copy(o_buf.at[buf], out_hbm.at[pl.ds(t * tile, tile), :], o_sem.at[buf])

    assert n_tiles >= 4

    step, slot = 0, 0
    x_copy(0, 0).start()
    x_copy(1, 1).start()
    x_copy(0, 0).wait()
    first_step()

    def steady_pass(t, carry):
        nonlocal step, slot
        step, slot = t, t % 2
        pl.when(t + 1 < n_tiles)(lambda: x_copy(t + 1, 1 - slot).start())
        x_copy(t, slot).wait()
        pl.when(t >= 4)(lambda: out_copy(t - 4, slot).wait())
        run("bc" "ab" "aab" "ab" "b" "aa" "bb" "ab")
        pl.when(t >= 2)(lambda: out_copy(t - 2, slot).start())
        return carry

    lax.fori_loop(1, n_tiles, steady_pass, 0)

    for t, stages in ((n_tiles, "bc" "bbbbbbb"), (n_tiles + 1, "c")):
        step, slot = t, t % 2
        out_copy(t - 4, slot).wait()
        run(stages)
        out_copy(t - 2, slot).start()
    for t in (n_tiles - 2, n_tiles - 1):
        out_copy(t, t % 2).wait()


def _resident(a):
    del a
    return pl.BlockSpec(memory_space=pltpu.VMEM)


def kernel(x, norm1_g, w_in, conv_w, gmlp_ln_g, gmlp_ln_b, gmlp_ws, gmlp_bs, out_norm_conv_g,
           out_norm_gmlp_g, w_out, norm2_g, w_up, w_down, final_g):
    depth = w_in.shape[0]
    batch, seq, d_model = x.shape
    c = conv_w.shape[1]
    gw = gmlp_ln_g.shape[1]
    head_dim = gw // GMLP_HEADS
    assert depth == 1, "one fused block per call"
    assert seq % TILE == 0 and TILE % GMLP_BLOCK == 0
    assert head_dim * HEADS_PER_VREG == LANES
    assert w_up.shape[2] % N_FF_CHUNKS == 0
    tiles_per_batch = seq // TILE
    n_tiles = batch * tiles_per_batch
    bf16 = jnp.bfloat16
    row = lambda a: a.reshape(1, -1)
    bias = jnp.repeat(jnp.transpose(gmlp_bs[0]), head_dim, axis=1)

    in_hbm = pl.BlockSpec(memory_space=pl.ANY)
    operands_and_specs = (
        (x.reshape(batch * seq, d_model), in_hbm),
        (row(norm1_g[0]), None), (w_in[0], in_hbm), (jnp.transpose(conv_w[0]), None),
        (row(gmlp_ln_g[0]), None), (row(gmlp_ln_b[0]), None), (gmlp_ws[0], None), (bias, None),
        (row(out_norm_conv_g[0]), None), (row(out_norm_gmlp_g[0]), None), (w_out[0], in_hbm),
        (row(norm2_g[0]), None), (w_up[0], in_hbm), (w_down[0], in_hbm), (row(final_g), None),
    )
    operands = [a for a, _ in operands_and_specs]
    in_specs = [_resident(a) if spec is None else spec for a, spec in operands_and_specs]
    out = pl.pallas_call(
        functools.partial(_block_kernel, n_tiles=n_tiles, tiles_per_batch=tiles_per_batch),
        in_specs=in_specs,
        out_specs=pl.BlockSpec(memory_space=pl.ANY),
        out_shape=jax.ShapeDtypeStruct((batch * seq, d_model), x.dtype),
        scratch_shapes=[
            pltpu.VMEM(w_in.shape[1:], bf16),
            pltpu.VMEM(w_out.shape[1:], bf16),
            pltpu.VMEM(w_up.shape[1:], bf16),
            pltpu.VMEM(w_down.shape[1:], bf16),
            pltpu.VMEM((GMLP_HEADS // HEADS_PER_VREG, GMLP_BLOCK, HEADS_PER_VREG * GMLP_BLOCK), bf16),
            pltpu.VMEM((TILE + SUBLANES, c), jnp.float32),
            pltpu.VMEM((TILE, gw), jnp.float32),
            pltpu.VMEM((TILE, d_model), jnp.float32),
            pltpu.VMEM((TILE, d_model), bf16),
            pltpu.VMEM((TILE, LANES), jnp.float32),
            pltpu.VMEM((TILE, d_model), jnp.float32),
            pltpu.VMEM((TILE, LANES), jnp.float32),
            pltpu.VMEM((2, TILE, d_model), jnp.float32),
            pltpu.VMEM((2, TILE, d_model), jnp.float32),
            pltpu.SemaphoreType.DMA((2,)),
            pltpu.SemaphoreType.DMA((2,)),
        ],
        compiler_params=pltpu.CompilerParams(vmem_limit_bytes=VMEM_LIMIT_BYTES),
        name="hybrid_block",
    )(*operands)
    return out.reshape(batch, seq, d_model)
```

```python
import functools

import jax
import jax.numpy as jnp
from jax import lax
from jax.experimental import pallas as pl
from jax.experimental.pallas import tpu as pltpu

EPS = 1e-6
CHUNK = 64
GMLP_BLOCK = 128
GMLP_HEADS = 8
LANES = 128
SUBLANES = 8
HEADS_PER_VREG = 2

TILE = 512
N_FF_CHUNKS = 4
VMEM_LIMIT_BYTES = 60000 * 1024

WEIGHT_CHUNK_BYTES = 1024 * 1024
WEIGHT_LOAD_SLOTS = 8
OVERLAPPED_LOAD_SLOTS = 5
OVERLAPPED_CHUNKS_PER_GROUP = 2


def _inv_rms(x):
    return lax.rsqrt(jnp.mean(x * x, axis=-1, keepdims=True) + EPS)


def _rms(x, g):
    return x * _inv_rms(x) * g


class _WeightStream:
    def __init__(self, src_hbm, dst_ref, stage_ref, sem_ref):
        self.src_hbm, self.dst_ref, self.stage_ref, self.sem_ref = src_hbm, dst_ref, stage_ref, sem_ref
        self.slots, self.rows, _ = stage_ref.shape
        assert src_hbm.shape[0] % self.rows == 0
        self.n_chunks = src_hbm.shape[0] // self.rows
        self.done = 0

    @staticmethod
    def staging(src_hbm, slots):
        n_cols = src_hbm.shape[1]
        rows = 1 << ((WEIGHT_CHUNK_BYTES // (n_cols * 4)).bit_length() - 1)
        return (pltpu.VMEM((slots, rows, n_cols), jnp.float32), pltpu.SemaphoreType.DMA((slots,)))

    def _copy(self, i, slot):
        return pltpu.make_async_copy(self.src_hbm.at[pl.ds(i * self.rows, self.rows), :],
                                     self.stage_ref.at[slot], self.sem_ref.at[slot])

    def start(self):
        for i in range(min(self.slots - 1, self.n_chunks)):
            self._copy(i, i).start()

    def advance(self, count):
        first, last = self.done, min(self.done + count, self.n_chunks)

        def step(i, carry):
            ahead = i + self.slots - 1

            @pl.when(ahead < self.n_chunks)
            def _():
                self._copy(ahead, ahead % self.slots).start()

            slot = i % self.slots
            self._copy(i, slot).wait()
            self.dst_ref[pl.ds(pl.multiple_of(i * self.rows, self.rows), self.rows), :] = (
                self.stage_ref[slot].astype(jnp.bfloat16))
            return carry

        lax.fori_loop(first, last, step, 0)
        self.done = last

    def finish(self):
        self.advance(self.n_chunks - self.done)


def _load_weight_as_bf16(src_hbm, dst_ref):
    def body(stage_ref, sem_ref):
        stream = _WeightStream(src_hbm, dst_ref, stage_ref, sem_ref)
        stream.start()
        stream.finish()

    pl.run_scoped(body, *_WeightStream.staging(src_hbm, WEIGHT_LOAD_SLOTS))


def _block_kernel(x_hbm, g1_ref, w_in_hbm, conv_w_ref, ln_g_ref, ln_b_ref, ws_ref, bias_ref,
                  gc_ref, gg_ref, w_out_hbm, g2_ref, w_up_hbm, w_down_hbm, gf_ref,
                  out_hbm, w_in_ref, w_out_ref, w_up_ref, w_down_ref,
                  ws_pair_ref, zbuf_ref, yb_ref, x1_ref, h2_ref, rr2_ref, pre_ref, ssq_ref,
                  x_buf, o_buf, x_sem, o_sem, *, n_tiles, tiles_per_batch):
    f32, bf16 = jnp.float32, jnp.bfloat16
    tile = TILE
    c = conv_w_ref.shape[1]
    gw = ln_g_ref.shape[1]
    ff_chunk = w_up_ref.shape[1] // N_FF_CHUNKS
    n_blocks = tile // GMLP_BLOCK
    n_pairs = GMLP_HEADS // HEADS_PER_VREG
    step = slot = None

    def prepare():
        qi = lax.broadcasted_iota(jnp.int32, (GMLP_BLOCK, GMLP_BLOCK), 0) // CHUNK
        kj = lax.broadcasted_iota(jnp.int32, (GMLP_BLOCK, GMLP_BLOCK), 1) // CHUNK
        causal = kj <= qi
        for h in range(GMLP_HEADS):
            g, r = divmod(h, HEADS_PER_VREG)
            ws_pair_ref[g, :, r * GMLP_BLOCK:(r + 1) * GMLP_BLOCK] = (
                jnp.where(causal, ws_ref[h], 0.0).astype(bf16))
        zbuf_ref[0:SUBLANES, :] = jnp.zeros((SUBLANES, c), f32)
        pre_ref[...] = jnp.zeros(pre_ref.shape, f32)
        ssq_ref[...] = jnp.zeros(ssq_ref.shape, f32)

    def stage_a():
        x = x_buf[slot]
        h = _rms(x, g1_ref[...]).astype(bf16)
        yield

        def in_proj(lo, width):
            return jnp.dot(h, w_in_ref[:, lo:lo + width], preferred_element_type=f32)

        z = in_proj(c, c) * in_proj(2 * c, c)
        zbuf_ref[SUBLANES:SUBLANES + tile, :] = z
        cw = conv_w_ref[...]
        conv = (zbuf_ref[SUBLANES - 2:SUBLANES - 2 + tile, :] * cw[0:1, :]
                + zbuf_ref[SUBLANES - 1:SUBLANES - 1 + tile, :] * cw[1:2, :]
                + z * cw[2:3, :])
        next_starts_sequence = (step + 1) % tiles_per_batch == 0
        zbuf_ref[0:SUBLANES, :] = jnp.where(next_starts_sequence, 0.0,
                                            zbuf_ref[tile:tile + SUBLANES, :])
        yield
        y_a = _rms(in_proj(0, c) * conv, gc_ref[...]).astype(bf16)
        yield

        v = in_proj(3 * c + gw, gw)
        mu = jnp.mean(v, axis=-1, keepdims=True)
        vc = v - mu
        vn = vc * lax.rsqrt(jnp.mean(vc * vc, axis=-1, keepdims=True) + EPS)
        vn = vn * ln_g_ref[...] + ln_b_ref[...]
        yield
        lane = lax.broadcasted_iota(jnp.int32, (tile, LANES), 1)
        low_head = lane < (LANES // HEADS_PER_VREG)
        for g in range(n_pairs):
            vg = vn[:, g * LANES:(g + 1) * LANES]
            v_lo = jnp.where(low_head, vg, 0.0).astype(bf16)
            v_hi = jnp.where(low_head, 0.0, vg).astype(bf16)
            rhs = jnp.concatenate(
                [jnp.concatenate([v_lo[n * GMLP_BLOCK:(n + 1) * GMLP_BLOCK, :]
                                  for n in range(n_blocks)], axis=1),
                 jnp.concatenate([v_hi[n * GMLP_BLOCK:(n + 1) * GMLP_BLOCK, :]
                                  for n in range(n_blocks)], axis=1)],
                axis=0)
            mixed = jnp.dot(ws_pair_ref[g], rhs, preferred_element_type=f32)
            for n in range(n_blocks):
                yb_ref[n * GMLP_BLOCK:(n + 1) * GMLP_BLOCK, g * LANES:(g + 1) * LANES] = (
                    mixed[:, n * LANES:(n + 1) * LANES])
        yield
        u = in_proj(3 * c, gw)
        bias = jnp.concatenate([bias_ref[...]] * n_blocks, axis=0)
        y_b = _rms(u * (yb_ref[...] + bias), gg_ref[...]).astype(bf16)
        yield

        y = jnp.concatenate([y_a, y_b], axis=-1)
        x1 = x + jnp.dot(y, w_out_ref[...], preferred_element_type=f32)
        x1_ref[...] = x1
        h2_ref[...] = (x1 * g2_ref[...]).astype(bf16)
        r2 = _inv_rms(x1)
        rr2_ref[...] = jnp.broadcast_to(r2 * r2, rr2_ref.shape)
        yield

    def stage_b():
        rr2 = jnp.concatenate([rr2_ref[...]] * (ff_chunk // LANES), axis=1)
        acc = None
        for k in range(N_FF_CHUNKS):
            cols = slice(k * ff_chunk, (k + 1) * ff_chunk)
            a = jnp.maximum(
                jnp.dot(h2_ref[...], w_up_ref[:, cols], preferred_element_type=f32), 0.0)
            a = (a * a * rr2).astype(bf16)
            yield
            if acc is None:
                acc = x1_ref[...]
            acc = acc + jnp.dot(a, w_down_ref[cols, :], preferred_element_type=f32)
            if k == N_FF_CHUNKS - 1:
                pre_ref[...] = acc
                sq = acc * acc
                ssq_ref[...] = sum(sq[:, j:j + LANES] for j in range(0, sq.shape[1], LANES))
            yield

    def stage_c():
        for r0 in range(0, tile, GMLP_BLOCK):
            rows = pl.ds(r0, GMLP_BLOCK)
            ms = jnp.sum(ssq_ref[rows, :], axis=-1, keepdims=True) * (1.0 / pre_ref.shape[1])
            o_buf[slot, rows, :] = pre_ref[rows, :] * lax.rsqrt(ms + EPS) * gf_ref[...]

    def run(order, after_group=None):
        b_groups = [i for i, name in enumerate(order) if name == "b"]
        if "a" in order and b_groups:
            assert order.rindex("a") > b_groups[0::2][-1]
        if "c" in order and b_groups:
            assert order.index("c") < b_groups[-1]
        gens = {}
        for name in order:
            if name == "c":
                stage_c()
                continue
            if name not in gens:
                gens[name] = {"a": stage_a, "b": stage_b}[name]()
            next(gens[name])
            if after_group is not None:
                after_group()
        for gen in gens.values():
            assert next(gen, "done") == "done"

    def first_step():
        _load_weight_as_bf16(w_in_hbm, w_in_ref)
        _load_weight_as_bf16(w_out_hbm, w_out_ref)
        prepare()

        def mix_while_loading(up_stage, up_sem, down_stage, down_sem):
            streams = (_WeightStream(w_up_hbm, w_up_ref, up_stage, up_sem),
                       _WeightStream(w_down_hbm, w_down_ref, down_stage, down_sem))
            for stream in streams:
                stream.start()

            def drain_some():
                for stream in streams:
                    stream.advance(OVERLAPPED_CHUNKS_PER_GROUP)

            run("aaaaaaa", after_group=drain_some)
            for stream in streams:
                stream.finish()

        pl.run_scoped(mix_while_loading,
                      *_WeightStream.staging(w_up_hbm, OVERLAPPED_LOAD_SLOTS),
                      *_WeightStream.staging(w_down_hbm, OVERLAPPED_LOAD_SLOTS))

    def x_copy(t, buf):
        return pltpu.make_async_copy(x_hbm.at[pl.ds(t * tile, tile), :], x_buf.at[buf], x_sem.at[buf])

    def out_copy(t, buf):
        return pltpu.make_async_copy(o_buf.at[buf], out_hbm.at[pl.ds(t * tile, tile), :], o_sem.at[buf])

    assert n_tiles >= 4

    step, slot = 0, 0
    x_copy(0, 0).start()
    x_copy(1, 1).start()
    x_copy(0, 0).wait()
    first_step()

    def steady_pass(t, carry):
        nonlocal step, slot
        step, slot = t, t % 2
        pl.when(t + 1 < n_tiles)(lambda: x_copy(t + 1, 1 - slot).start())
        x_copy(t, slot).wait()
        pl.when(t >= 4)(lambda: out_copy(t - 4, slot).wait())
        run("bc" "ab" "aab" "ab" "b" "aa" "bb" "ab")
        pl.when(t >= 2)(lambda: out_copy(t - 2, slot).start())
        return carry

    lax.fori_loop(1, n_tiles, steady_pass, 0)

    for t, stages in ((n_tiles, "bc" "bbbbbbb"), (n_tiles + 1, "c")):
        step, slot = t, t % 2
        out_copy(t - 4, slot).wait()
        run(stages)
        out_copy(t - 2, slot).start()
    for t in (n_tiles - 2, n_tiles - 1):
        out_copy(t, t % 2).wait()


def _resident(a):
    del a
    return pl.BlockSpec(memory_space=pltpu.VMEM)


def kernel(x, norm1_g, w_in, conv_w, gmlp_ln_g, gmlp_ln_b, gmlp_ws, gmlp_bs, out_norm_conv_g,
           out_norm_gmlp_g, w_out, norm2_g, w_up, w_down, final_g):
    depth = w_in.shape[0]
    batch, seq, d_model = x.shape
    c = conv_w.shape[1]
    gw = gmlp_ln_g.shape[1]
    head_dim = gw // GMLP_HEADS
    assert depth == 1, "one fused block per call"
    assert seq % TILE == 0 and TILE % GMLP_BLOCK == 0
    assert head_dim * HEADS_PER_VREG == LANES
    assert w_up.shape[2] % N_FF_CHUNKS == 0
    tiles_per_batch = seq // TILE
    n_tiles = batch * tiles_per_batch
    bf16 = jnp.bfloat16
    row = lambda a: a.reshape(1, -1)
    bias = jnp.repeat(jnp.transpose(gmlp_bs[0]), head_dim, axis=1)

    in_hbm = pl.BlockSpec(memory_space=pl.ANY)
    operands_and_specs = (
        (x.reshape(batch * seq, d_model), in_hbm),
        (row(norm1_g[0]), None), (w_in[0], in_hbm), (jnp.transpose(conv_w[0]), None),
        (row(gmlp_ln_g[0]), None), (row(gmlp_ln_b[0]), None), (gmlp_ws[0], None), (bias, None),
        (row(out_norm_conv_g[0]), None), (row(out_norm_gmlp_g[0]), None), (w_out[0], in_hbm),
        (row(norm2_g[0]), None), (w_up[0], in_hbm), (w_down[0], in_hbm), (row(final_g), None),
    )
    operands = [a for a, _ in operands_and_specs]
    in_specs = [_resident(a) if spec is None else spec for a, spec in operands_and_specs]
    out = pl.pallas_call(
        functools.partial(_block_kernel, n_tiles=n_tiles, tiles_per_batch=tiles_per_batch),
        in_specs=in_specs,
        out_specs=pl.BlockSpec(memory_space=pl.ANY),
        out_shape=jax.ShapeDtypeStruct((batch * seq, d_model), x.dtype),
        scratch_shapes=[
            pltpu.VMEM(w_in.shape[1:], bf16),
            pltpu.VMEM(w_out.shape[1:], bf16),
            pltpu.VMEM(w_up.shape[1:], bf16),
            pltpu.VMEM(w_down.shape[1:], bf16),
            pltpu.VMEM((GMLP_HEADS // HEADS_PER_VREG, GMLP_BLOCK, HEADS_PER_VREG * GMLP_BLOCK), bf16),
            pltpu.VMEM((TILE + SUBLANES, c), jnp.float32),
            pltpu.VMEM((TILE, gw), jnp.float32),
            pltpu.VMEM((TILE, d_model), jnp.float32),
            pltpu.VMEM((TILE, d_model), bf16),
            pltpu.VMEM((TILE, LANES), jnp.float32),
            pltpu.VMEM((TILE, d_model), jnp.float32),
            pltpu.VMEM((TILE, LANES), jnp.float32),
            pltpu.VMEM((2, TILE, d_model), jnp.float32),
            pltpu.VMEM((2, TILE, d_model), jnp.float32),
            pltpu.SemaphoreType.DMA((2,)),
            pltpu.SemaphoreType.DMA((2,)),
        ],
        compiler_params=pltpu.CompilerParams(vmem_limit_bytes=VMEM_LIMIT_BYTES),
        name="hybrid_block",
    )(*operands)
    return out.reshape(batch, seq, d_model)
```
